```python
import math, functools
import jax, jax.numpy as jnp
from jax import lax
import numpy as np

D_MODEL = 2048
BATCH = 4
SEQ = 2048
DEPTH = 1
DEC_BATCH = 128
DEC_SEQ = 4
PAST_LEN = 2048
PAGE_SIZE = 128

HEAD_DIM = 128
H_FOX = 8
H_DSA = 8
W_FOX = H_FOX * HEAD_DIM
W_DSA = H_DSA * HEAD_DIM
H_IDX = 16
D_IDX = 64
TOPK_MAX = 256
N_BUCKETS = 32
MAX_DISTANCE = 128
Q_BLOCK = 128
PEER_HEADS = 8
PEER_NKEYS = 128
PEER_EXPERTS = PEER_NKEYS * PEER_NKEYS
PEER_DKEY = 128
PEER_TOPK = 16
PEER_TOKEN_BLOCK = 128
NORM_EPS = 1e-6

IN_SPLITS = (W_FOX, W_FOX, W_FOX, H_FOX,
             W_DSA, W_DSA, W_DSA,
             H_IDX * D_IDX, D_IDX, H_IDX,
             D_MODEL, D_MODEL)
N_IN = 3 * W_FOX + H_FOX + 3 * W_DSA + H_IDX * D_IDX + D_IDX + H_IDX + 2 * D_MODEL

kernel_name = "fox_dsa_peer_hybrid_step"


def rms_norm(x, gain):
    xf = x.astype(jnp.float32)
    y = xf * lax.rsqrt(jnp.mean(xf * xf, axis=-1, keepdims=True) + NORM_EPS)
    return (y * gain.astype(jnp.float32)).astype(x.dtype)


def t5_bucket(rel):
    n = jnp.maximum(rel, 0)
    max_exact = N_BUCKETS // 2
    nf = jnp.maximum(n, 1).astype(jnp.float32)
    large = max_exact + (jnp.log(nf / max_exact) / math.log(MAX_DISTANCE / max_exact)
                         * (N_BUCKETS - max_exact)).astype(jnp.int32)
    large = jnp.minimum(large, N_BUCKETS - 1)
    return jnp.where(n < max_exact, n, large)


def rows_at(a, sel):
    return jax.vmap(lambda r, s: r[s])(a, sel)


def gather_pages(pool, page_table):
    g = pool[page_table]
    return g.reshape((g.shape[0], g.shape[1] * g.shape[2]) + pool.shape[2:])


def paged_rows(pool, new_rows, page_table, sel):
    n_pages = page_table.shape[1]
    past_len = n_pages * PAGE_SIZE
    page = jnp.minimum(sel // PAGE_SIZE, n_pages - 1)
    phys = jax.vmap(lambda pt, pg: pt[pg])(page_table, page)
    past = pool[phys, sel % PAGE_SIZE]
    new = rows_at(new_rows, jnp.clip(sel - past_len, 0, new_rows.shape[1] - 1))
    is_new = (sel >= past_len).reshape(sel.shape + (1,) * (pool.ndim - 2))
    return jnp.where(is_new, new, past)


def _gather_contig(k, v, sel):
    return rows_at(k, sel), rows_at(v, sel)


def _gather_paged_kv(pool_k, pool_v, k_new, v_new, page_table, sel):
    return paged_rows(pool_k, k_new, page_table, sel), paged_rows(pool_v, v_new, page_table, sel)


def sweep_query_blocks(fn, qs, blk):
    b, t = qs[0].shape[:2]
    pad = (-t) % blk
    nb = (t + pad) // blk
    blocks = []
    for a in qs:
        a = jnp.pad(a, [(0, 0), (0, pad)] + [(0, 0)] * (a.ndim - 2))
        blocks.append(jnp.moveaxis(a.reshape((b, nb, blk) + a.shape[2:]), 1, 0))
    out = lax.map(lambda bs: fn(*bs), tuple(blocks))
    out = jnp.moveaxis(out, 0, 1)
    return out.reshape((b, nb * blk) + out.shape[3:])[:, :t]


def project_inputs(h, w_in, b_forget, b_gate, qk_gain):
    b, t, _ = h.shape
    proj = jnp.einsum('btd,dn->btn', h, w_in)
    offs = [int(o) for o in np.cumsum(IN_SPLITS)[:-1]]
    qa, ka, va, fa, qb, kb, vb, qi, ki, wi, ga, gb = jnp.split(proj, offs, axis=-1)
    qa = rms_norm(qa.reshape(b, t, H_FOX, HEAD_DIM), qk_gain[0])
    ka = rms_norm(ka.reshape(b, t, H_FOX, HEAD_DIM), qk_gain[1])
    va = va.reshape(b, t, H_FOX, HEAD_DIM)
    logf = jax.nn.log_sigmoid((fa + b_forget).astype(jnp.float32))
    qb = rms_norm(qb.reshape(b, t, H_DSA, HEAD_DIM), qk_gain[2])
    kb = rms_norm(kb.reshape(b, t, H_DSA, HEAD_DIM), qk_gain[3])
    vb = vb.reshape(b, t, H_DSA, HEAD_DIM)
    qi = qi.reshape(b, t, H_IDX, D_IDX)
    ga = jax.nn.sigmoid(ga + b_gate[0])
    gb = jax.nn.sigmoid(gb + b_gate[1])
    return qa, ka, va, logf, qb, kb, vb, qi, ki, wi, ga, gb


def fox_attention(q, k, v, c_q, c_k, pos_q, pos_k):
    scale = HEAD_DIM ** -0.5
    ck = jnp.swapaxes(c_k.astype(jnp.float32), 1, 2)[:, :, None, :]

    def block(qb, cqb, pb):
        s = jnp.einsum('bqhd,bkhd->bhqk', qb, k).astype(jnp.float32) * scale
        s = s + jnp.swapaxes(cqb.astype(jnp.float32), 1, 2)[..., None] - ck
        mask = pos_k[None, None, None, :] <= pb[:, None, :, None]
        p = jax.nn.softmax(jnp.where(mask, s, -jnp.inf), axis=-1)
        return jnp.einsum('bhqk,bkhd->bqhd', p.astype(v.dtype), v)

    return sweep_query_blocks(block, (q, c_q, pos_q), min(Q_BLOCK, q.shape[1]))


def dsa_attention(q, qi, wi, pos_q, k_idx, pos_k, gather_kv, rel_bias):
    topk = min(TOPK_MAX, pos_k.shape[0] // 4)
    scale = HEAD_DIM ** -0.5
    idx_scale = (H_IDX * D_IDX) ** -0.5

    def block(qb, qib, wib, pb):
        sc = jax.nn.relu(jnp.einsum('bqhi,bki->bqhk', qib, k_idx).astype(jnp.float32))
        score = jnp.einsum('bqhk,bqh->bqk', sc, wib.astype(jnp.float32)) * idx_scale
        admissible = pos_k[None, None, :] <= pb[..., None]
        _, sel = lax.top_k(jnp.where(admissible, score, -jnp.inf), topk)
        pos_sel = pos_k[sel]
        k_sel, v_sel = gather_kv(sel)
        s = jnp.einsum('bqhd,bqkhd->bhqk', qb, k_sel).astype(jnp.float32) * scale
        bias = rel_bias[t5_bucket(pb[..., None] - pos_sel)].astype(jnp.float32)
        s = s + jnp.transpose(bias, (0, 3, 1, 2))
        valid = (pos_sel <= pb[..., None])[:, None]
        p = jax.nn.softmax(jnp.where(valid, s, -jnp.inf), axis=-1)
        return jnp.einsum('bhqk,bqkhd->bqhd', p.astype(v_sel.dtype), v_sel)

    return sweep_query_blocks(block, (q, qi, wi, pos_q), min(Q_BLOCK, q.shape[1]))


def merge_branches(x, o_a, o_b, g_a, g_b, w_branch_a, w_branch_b, w_out):
    b, t = x.shape[:2]
    y_a = jnp.einsum('btf,fd->btd', o_a.reshape(b, t, -1), w_branch_a)
    y_b = jnp.einsum('btf,fd->btd', o_b.reshape(b, t, -1), w_branch_b)
    return x + jnp.einsum('btd,de->bte', g_a * y_a + g_b * y_b, w_out)


def peer_ffn(h, w_query, sub_keys, u_tab, v_tab):
    b, t, d = h.shape
    x = h.reshape(b * t, d)
    n = x.shape[0]
    pad = (-n) % PEER_TOKEN_BLOCK
    xb = jnp.pad(x, ((0, pad), (0, 0))).reshape(-1, PEER_TOKEN_BLOCK, d)

    def block(xt):
        m = xt.shape[0]
        q = (xt @ w_query).reshape(m, PEER_HEADS, 2, PEER_DKEY // 2)
        s = jnp.einsum('nhcd,hckd->nhck', q, sub_keys).astype(jnp.float32)
        sv, si = lax.top_k(s, PEER_TOPK)
        cand = (sv[:, :, 0, :, None] + sv[:, :, 1, None, :]).reshape(m, PEER_HEADS, -1)
        cand_idx = (si[:, :, 0, :, None] * PEER_NKEYS + si[:, :, 1, None, :]).reshape(m, PEER_HEADS, -1)
        top_v, top_pos = lax.top_k(cand, PEER_TOPK)
        idx = jnp.take_along_axis(cand_idx, top_pos, axis=-1)
        g = jax.nn.softmax(top_v, axis=-1)
        act = jax.nn.gelu(jnp.einsum('nd,nhkd->nhk', xt, u_tab[idx]).astype(jnp.float32), approximate=False)
        return jnp.einsum('nhk,nhkd->nd', (g * act).astype(v_tab.dtype), v_tab[idx])

    out = lax.map(block, xb).reshape(-1, d)[:n]
    return out.reshape(b, t, d)


def setup_inputs(seed: int = 0) -> dict:
    key = jax.random.key(seed)
    ks = jax.random.split(key, 24)
    f32 = jnp.float32
    n_pages = PAST_LEN // PAGE_SIZE
    n_phys = (DEC_BATCH * n_pages * 5) // 4

    def nrm(k, shape, s):
        return s * jax.random.normal(k, shape, f32)

    page_table = jax.random.permutation(ks[0], n_phys)[:DEC_BATCH * n_pages]
    page_table = page_table.reshape(DEC_BATCH, n_pages).astype(jnp.int32)
    kv_shape = (DEPTH, n_phys, PAGE_SIZE, H_FOX, HEAD_DIM)
    return {
        "x_prompt": nrm(ks[1], (BATCH, SEQ, D_MODEL), 1.0),
        "x_sample": nrm(ks[2], (DEC_BATCH, DEC_SEQ, D_MODEL), 1.0),
        "cache_fox_k": nrm(ks[3], kv_shape, 1.0),
        "cache_fox_v": nrm(ks[4], kv_shape, 1.0),
        "cache_fox_logf": jax.nn.log_sigmoid(2.0 + nrm(ks[5], (DEPTH, n_phys, PAGE_SIZE, H_FOX), 0.5)),
        "cache_dsa_k": nrm(ks[6], (DEPTH, n_phys, PAGE_SIZE, H_DSA, HEAD_DIM), 1.0),
        "cache_dsa_v": nrm(ks[7], (DEPTH, n_phys, PAGE_SIZE, H_DSA, HEAD_DIM), 1.0),
        "cache_idx_k": nrm(ks[8], (DEPTH, n_phys, PAGE_SIZE, D_IDX), 1.0),
        "page_table": page_table,
        "norm_mix": 1.0 + nrm(ks[9], (DEPTH, D_MODEL), 0.01),
        "w_in": nrm(ks[10], (DEPTH, D_MODEL, N_IN), D_MODEL ** -0.5),
        "b_forget": 2.0 + nrm(ks[11], (DEPTH, H_FOX), 0.5),
        "b_gate": nrm(ks[12], (DEPTH, 2, D_MODEL), 0.01),
        "qk_gain": 1.0 + nrm(ks[13], (DEPTH, 4, HEAD_DIM), 0.01),
        "w_branch_a": nrm(ks[14], (DEPTH, W_FOX, D_MODEL), W_FOX ** -0.5),
        "w_branch_b": nrm(ks[15], (DEPTH, W_DSA, D_MODEL), W_DSA ** -0.5),
        "w_out": nrm(ks[16], (DEPTH, D_MODEL, D_MODEL), D_MODEL ** -0.5),
        "rel_bias": nrm(ks[17], (N_BUCKETS, H_DSA), 0.5),
        "norm_ffn": 1.0 + nrm(ks[18], (DEPTH, D_MODEL), 0.01),
        "peer_w_query": nrm(ks[19], (DEPTH, D_MODEL, PEER_HEADS * PEER_DKEY), D_MODEL ** -0.5),
        "peer_sub_keys": nrm(ks[20], (DEPTH, PEER_HEADS, 2, PEER_NKEYS, PEER_DKEY // 2), (PEER_DKEY // 2) ** -0.5),
        "peer_u": nrm(ks[21], (DEPTH, PEER_EXPERTS, D_MODEL), D_MODEL ** -0.5),
        "peer_v": nrm(ks[22], (DEPTH, PEER_EXPERTS, D_MODEL), 0.5),
    }


def reference(x_prompt, x_sample, cache_fox_k, cache_fox_v, cache_fox_logf, cache_dsa_k, cache_dsa_v,
              cache_idx_k, page_table, norm_mix, w_in, b_forget, b_gate, qk_gain, w_branch_a, w_branch_b,
              w_out, rel_bias, norm_ffn, peer_w_query, peer_sub_keys, peer_u, peer_v):
    bp, tp = x_prompt.shape[:2]
    bs, ts = x_sample.shape[:2]
    past_len = page_table.shape[1] * PAGE_SIZE
    pos_kp = jnp.arange(tp, dtype=jnp.int32)
    pos_qp = jnp.broadcast_to(pos_kp, (bp, tp))
    pos_ks = jnp.arange(past_len + ts, dtype=jnp.int32)
    pos_qs = jnp.broadcast_to(pos_ks[past_len:], (bs, ts))

    xp, xs = x_prompt, x_sample
    p_rows = ([], [], [], [], [], [])
    s_rows = ([], [], [], [], [], [])
    for l in range(DEPTH):
        qa, ka, va, logf, qb, kb, vb, qi, ki, wi, ga, gb = project_inputs(
            rms_norm(xp, norm_mix[l]), w_in[l], b_forget[l], b_gate[l], qk_gain[l])
        c = jnp.cumsum(logf, axis=1)
        oa = fox_attention(qa, ka, va, c, c, pos_qp, pos_kp)
        ob = dsa_attention(qb, qi, wi, pos_qp, ki, pos_kp, functools.partial(_gather_contig, kb, vb), rel_bias)
        xp = merge_branches(xp, oa, ob, ga, gb, w_branch_a[l], w_branch_b[l], w_out[l])
        xp = xp + peer_ffn(rms_norm(xp, norm_ffn[l]), peer_w_query[l], peer_sub_keys[l], peer_u[l], peer_v[l])
        for lst, a in zip(p_rows, (ka, va, logf, kb, vb, ki)):
            lst.append(a)

        qa_s, ka_s, va_s, logf_s, qb_s, kb_s, vb_s, qi_s, ki_s, wi_s, ga_s, gb_s = project_inputs(
            rms_norm(xs, norm_mix[l]), w_in[l], b_forget[l], b_gate[l], qk_gain[l])
        ka_all = jnp.concatenate([gather_pages(cache_fox_k[l], page_table), ka_s], axis=1)
        va_all = jnp.concatenate([gather_pages(cache_fox_v[l], page_table), va_s], axis=1)
        logf_all = jnp.concatenate(
            [gather_pages(cache_fox_logf[l], page_table).astype(jnp.float32), logf_s], axis=1)
        c_all = jnp.cumsum(logf_all, axis=1)
        oa_s = fox_attention(qa_s, ka_all, va_all, c_all[:, past_len:], c_all, pos_qs, pos_ks)
        ki_all = jnp.concatenate([gather_pages(cache_idx_k[l], page_table), ki_s], axis=1)
        gather_s = functools.partial(_gather_paged_kv, cache_dsa_k[l], cache_dsa_v[l], kb_s, vb_s, page_table)
        ob_s = dsa_attention(qb_s, qi_s, wi_s, pos_qs, ki_all, pos_ks, gather_s, rel_bias)
        xs = merge_branches(xs, oa_s, ob_s, ga_s, gb_s, w_branch_a[l], w_branch_b[l], w_out[l])
        xs = xs + peer_ffn(rms_norm(xs, norm_ffn[l]), peer_w_query[l], peer_sub_keys[l], peer_u[l], peer_v[l])
        for lst, a in zip(s_rows, (ka_s, va_s, logf_s, kb_s, vb_s, ki_s)):
            lst.append(a)

    return (xp, xs,
            jnp.stack(p_rows[0]), jnp.stack(p_rows[1]), jnp.stack(p_rows[2]),
            jnp.stack(p_rows[3]), jnp.stack(p_rows[4]), jnp.stack(p_rows[5]),
            jnp.stack(s_rows[0]), jnp.stack(s_rows[1]), jnp.stack(s_rows[2]),
            jnp.stack(s_rows[3]), jnp.stack(s_rows[4]), jnp.stack(s_rows[5]))
```

```python
import functools
import math

import jax
import jax.numpy as jnp
import numpy as np
from jax import lax
from jax.experimental import pallas as pl
from jax.experimental.pallas import tpu as pltpu

HEAD_DIM = 128
N_HEADS = 8
WIDTH = N_HEADS * HEAD_DIM
H_IDX = 16
D_IDX = 64
TOPK_MAX = 256
N_BUCKETS = 32
MAX_DISTANCE = 128
PAGE = 128
PEER_HEADS = 8
PEER_NKEYS = 128
PEER_HALF = 64
PEER_TOPK = 16
NORM_EPS = 1e-6
NEG = -1e30
INT_MIN = -2 ** 31
SMALL_W = 128
KI_OFF, WI_OFF, FA_OFF = 0, 64, 80

VMEM_LIMIT = 56 * 1024 * 1024

_NT = (((1,), (1,)), ((), ()))


def _params(sem):
    return pltpu.CompilerParams(dimension_semantics=sem, vmem_limit_bytes=VMEM_LIMIT)


def _bf(x):
    return x.astype(jnp.bfloat16)


def _rmsnorm_kernel(x_ref, g_ref, o_ref):
    x = x_ref[...]
    ms = jnp.mean(x * x, axis=-1, keepdims=True)
    o_ref[...] = (x * lax.rsqrt(ms + NORM_EPS) * g_ref[...]).astype(o_ref.dtype)


def rmsnorm_bf16(x, gain, tm):
    n, d = x.shape
    return pl.pallas_call(
        _rmsnorm_kernel,
        grid=(n // tm,),
        in_specs=[pl.BlockSpec((tm, d), lambda i: (i, 0)),
                  pl.BlockSpec((1, d), lambda i: (0, 0))],
        out_specs=pl.BlockSpec((tm, d), lambda i: (i, 0)),
        out_shape=jax.ShapeDtypeStruct((n, d), jnp.bfloat16),
        compiler_params=_params(("parallel",)),
        name="rmsnorm",
    )(x, gain.reshape(1, d))


def _head_rmsnorm(y, gain):
    outs = []
    for h in range(N_HEADS):
        yh = y[:, h * HEAD_DIM:(h + 1) * HEAD_DIM]
        ms = jnp.mean(yh * yh, axis=-1, keepdims=True)
        outs.append(yh * lax.rsqrt(ms + NORM_EPS) * gain[:, h * HEAD_DIM:(h + 1) * HEAD_DIM])
    return jnp.concatenate(outs, axis=-1)


def _qkv_kernel(h_ref, w_ref, g_ref, q_ref, k32_ref, k16_ref, v32_ref, v16_ref):
    j = pl.program_id(1)
    y = jnp.dot(h_ref[...], w_ref[...], preferred_element_type=jnp.float32)

    @pl.when(j == 0)
    def _():
        q_ref[...] = _bf(_head_rmsnorm(y, g_ref[0:1, :]))

    @pl.when(j == 1)
    def _():
        k = _head_rmsnorm(y, g_ref[1:2, :])
        k32_ref[...] = k
        k16_ref[...] = _bf(k)

    @pl.when(j == 2)
    def _():
        v32_ref[...] = y
        v16_ref[...] = _bf(y)


def project_qkv(h, w, gains, tm):
    n, d = h.shape
    row = pl.BlockSpec((tm, WIDTH), lambda i, j: (i, 0))
    f32 = jax.ShapeDtypeStruct((n, WIDTH), jnp.float32)
    b16 = jax.ShapeDtypeStruct((n, WIDTH), jnp.bfloat16)
    return pl.pallas_call(
        _qkv_kernel,
        grid=(n // tm, 3),
        in_specs=[pl.BlockSpec((tm, d), lambda i, j: (i, 0)),
                  pl.BlockSpec((d, WIDTH), lambda i, j: (0, j)),
                  pl.BlockSpec((2, WIDTH), lambda i, j: (0, 0))],
        out_specs=[row, row, row, row, row],
        out_shape=[b16, f32, b16, f32, b16],
        compiler_params=_params(("parallel", "arbitrary")),
        name="project_qkv",
    )(h, w, gains)


def _log_sigmoid(x):
    return jnp.minimum(x, 0.0) - jnp.log(1.0 + jnp.exp(-jnp.abs(x)))


def _proj_kernel(h_ref, w_ref, b_ref, o_ref, *, mode):
    y = jnp.dot(h_ref[...], w_ref[...], preferred_element_type=jnp.float32)
    if mode == "sigmoid":
        y = jax.nn.sigmoid(y + b_ref[...])
    elif mode == "small":
        lane = lax.broadcasted_iota(jnp.int32, y.shape, 1)
        is_f = (lane >= FA_OFF) & (lane < FA_OFF + N_HEADS)
        y = jnp.where(is_f, _log_sigmoid(y + b_ref[...]), y)
    o_ref[...] = y.astype(o_ref.dtype)


def project(h, w, bias, tm, tn, mode, out_dtype):
    n, d = h.shape
    nc = w.shape[1]
    return pl.pallas_call(
        functools.partial(_proj_kernel, mode=mode),
        grid=(n // tm, nc // tn),
        in_specs=[pl.BlockSpec((tm, d), lambda i, j: (i, 0)),
                  pl.BlockSpec((d, tn), lambda i, j: (0, j)),
                  pl.BlockSpec((1, tn), lambda i, j: (0, j))],
        out_specs=pl.BlockSpec((tm, tn), lambda i, j: (i, j)),
        out_shape=jax.ShapeDtypeStruct((n, nc), out_dtype),
        compiler_params=_params(("parallel", "arbitrary")),
        name="project_" + mode,
    )(h, w, bias)


def _online_softmax_step(s, h, m_ref, l_ref, acc_ref, v_h):
    m_old = m_ref[h]
    m_new = jnp.maximum(m_old, jnp.max(s, axis=-1, keepdims=True))
    alpha = jnp.exp(m_old - m_new)
    p = jnp.exp(s - m_new)
    l_ref[h] = alpha * l_ref[h] + jnp.sum(p, axis=-1, keepdims=True)
    acc_ref[h] = alpha * acc_ref[h] + jnp.dot(_bf(p), v_h, preferred_element_type=jnp.float32)
    m_ref[h] = m_new


def _flash_init(m_ref, l_ref, acc_ref):
    m_ref[...] = jnp.full(m_ref.shape, NEG, jnp.float32)
    l_ref[...] = jnp.zeros(l_ref.shape, jnp.float32)
    acc_ref[...] = jnp.zeros(acc_ref.shape, jnp.float32)


def _flash_finish(o_ref, l_ref, acc_ref):
    for h in range(N_HEADS):
        o_ref[:, h * HEAD_DIM:(h + 1) * HEAD_DIM] = (acc_ref[h] / l_ref[h]).astype(o_ref.dtype)


def _fox_prompt_kernel(q_ref, k_ref, v_ref, negc_ref, o_ref, m_ref, l_ref, acc_ref, *, tq, tk):
    qi, ki = pl.program_id(1), pl.program_id(2)

    @pl.when(ki == 0)
    def _():
        _flash_init(m_ref, l_ref, acc_ref)

    @pl.when(ki <= qi)
    def _():
        row = qi * tq + lax.broadcasted_iota(jnp.int32, (tq, tk), 0)
        col = ki * tk + lax.broadcasted_iota(jnp.int32, (tq, tk), 1)
        causal = col <= row
        scale = HEAD_DIM ** -0.5
        for h in range(N_HEADS):
            sl = slice(h * HEAD_DIM, (h + 1) * HEAD_DIM)
            s = lax.dot_general(q_ref[:, sl], k_ref[:, sl], _NT, preferred_element_type=jnp.float32)
            s = jnp.where(causal, s * scale + negc_ref[h:h + 1, :], NEG)
            _online_softmax_step(s, h, m_ref, l_ref, acc_ref, v_ref[:, sl])

    @pl.when(ki == pl.num_programs(2) - 1)
    def _():
        _flash_finish(o_ref, l_ref, acc_ref)


def _dsa_prompt_kernel(q_ref, k_ref, v_ref, mask_ref, t5_ref, o_ref, m_ref, l_ref, acc_ref):
    qi, ki = pl.program_id(1), pl.program_id(2)

    @pl.when(ki == 0)
    def _():
        _flash_init(m_ref, l_ref, acc_ref)

    @pl.when(ki <= qi)
    def _():
        off = jnp.minimum(qi - ki, 2)
        mask = mask_ref[...].astype(jnp.float32)
        scale = HEAD_DIM ** -0.5
        for h in range(N_HEADS):
            sl = slice(h * HEAD_DIM, (h + 1) * HEAD_DIM)
            s = lax.dot_general(q_ref[:, sl], k_ref[:, sl], _NT, preferred_element_type=jnp.float32)
            s = s * scale + t5_ref[off, h] + mask
            _online_softmax_step(s, h, m_ref, l_ref, acc_ref, v_ref[:, sl])

    @pl.when(ki == pl.num_programs(2) - 1)
    def _():
        _flash_finish(o_ref, l_ref, acc_ref)


def _flash_scratch(tq):
    return [pltpu.VMEM((N_HEADS, tq, 1), jnp.float32),
            pltpu.VMEM((N_HEADS, tq, 1), jnp.float32),
            pltpu.VMEM((N_HEADS, tq, HEAD_DIM), jnp.float32)]


def fox_prompt_attention(q, k, v, negc, batch, seq, t):
    nb = seq // t
    qspec = pl.BlockSpec((t, WIDTH), lambda b, i, j: (b * nb + i, 0))
    kspec = pl.BlockSpec((t, WIDTH), lambda b, i, j: (b * nb + jnp.minimum(i, j), 0))
    return pl.pallas_call(
        functools.partial(_fox_prompt_kernel, tq=t, tk=t),
        grid=(batch, nb, nb),
        in_specs=[qspec, kspec, kspec,
                  pl.BlockSpec((None, N_HEADS, t), lambda b, i, j: (b, 0, jnp.minimum(i, j)))],
        out_specs=qspec,
        out_shape=jax.ShapeDtypeStruct((batch * seq, WIDTH), jnp.bfloat16),
        scratch_shapes=_flash_scratch(t),
        compiler_params=_params(("parallel", "parallel", "arbitrary")),
        name="fox_prompt",
    )(q, k, v, negc)


def dsa_prompt_attention(q, k, v, mask, t5, batch, seq, t):
    nb = seq // t
    qspec = pl.BlockSpec((t, WIDTH), lambda b, i, j: (b * nb + i, 0))
    kspec = pl.BlockSpec((t, WIDTH), lambda b, i, j: (b * nb + jnp.minimum(i, j), 0))
    return pl.pallas_call(
        _dsa_prompt_kernel,
        grid=(batch, nb, nb),
        in_specs=[qspec, kspec, kspec,
                  pl.BlockSpec((t, t), lambda b, i, j: (b * nb + i, jnp.minimum(i, j))),
                  pl.BlockSpec((3, N_HEADS, t, t), lambda b, i, j: (0, 0, 0, 0))],
        out_specs=qspec,
        out_shape=jax.ShapeDtypeStruct((batch * seq, WIDTH), jnp.bfloat16),
        scratch_shapes=_flash_scratch(t),
        compiler_params=_params(("parallel", "parallel", "arbitrary")),
        name="dsa_prompt",
    )(q, k, v, mask, t5)


def _sortable_key(score):
    bits = pltpu.bitcast(score, jnp.int32)
    return jnp.where(bits < 0, bits ^ jnp.int32(0x7FFFFFFF), bits)


def _kth_largest_key(key, k):
    kf = jnp.float32(k)

    def count_ge(cand):
        return jnp.sum(jnp.where(key >= cand, 1.0, 0.0), axis=-1, keepdims=True)

    t = jnp.full(key.shape[:-1] + (1,), INT_MIN, jnp.int32)
    cand = jnp.zeros_like(t)
    t = jnp.where(count_ge(cand) >= kf, cand, t)

    def body(it, t):
        cand = t | jnp.left_shift(jnp.int32(1), 30 - it)
        return jnp.where(count_ge(cand) >= kf, cand, t)

    return lax.fori_loop(0, 31, body, t)


def _dsa_prompt_select_kernel(qi_ref, wq_ref, kk_ref, o_ref, *, tq, topk):
    i = pl.program_id(1)
    seq = kk_ref.shape[0]
    ki = _bf(kk_ref[:, KI_OFF:KI_OFF + D_IDX])
    wi = wq_ref[:, WI_OFF:WI_OFF + H_IDX]
    score = jnp.zeros((tq, seq), jnp.float32)
    for h in range(H_IDX):
        z = lax.dot_general(qi_ref[:, h * D_IDX:(h + 1) * D_IDX], ki, _NT,
                            preferred_element_type=jnp.float32)
        score = score + jnp.maximum(z, 0.0) * wi[:, h:h + 1]
    score = score * ((H_IDX * D_IDX) ** -0.5)
    row = i * tq + lax.broadcasted_iota(jnp.int32, (tq, seq), 0)
    col = lax.broadcasted_iota(jnp.int32, (tq, seq), 1)
    admissible = col <= row
    key = jnp.where(admissible, _sortable_key(score), INT_MIN)
    thr = _kth_largest_key(key, topk)
    o_ref[...] = jnp.where(admissible & (key >= thr), 0.0, NEG).astype(o_ref.dtype)


def dsa_prompt_select(qi, small, batch, seq, tq):
    nb = seq // tq
    topk = min(TOPK_MAX, seq // 4)
    return pl.pallas_call(
        functools.partial(_dsa_prompt_select_kernel, tq=tq, topk=topk),
        grid=(batch, nb),
        in_specs=[pl.BlockSpec((tq, H_IDX * D_IDX), lambda b, i: (b * nb + i, 0)),
                  pl.BlockSpec((tq, SMALL_W), lambda b, i: (b * nb + i, 0)),
                  pl.BlockSpec((seq, SMALL_W), lambda b, i: (b, 0))],
        out_specs=pl.BlockSpec((tq, seq), lambda b, i: (b * nb + i, 0)),
        out_shape=jax.ShapeDtypeStruct((batch * seq, seq), jnp.bfloat16),
        compiler_params=_params(("parallel", "arbitrary")),
        name="dsa_prompt_select",
    )(qi, small, small)


QPAD = 8


def _paged_attention_kernel(pt_ref, q_ref, k_ref, v_ref, kn_ref, vn_ref, cb_ref, cbn_ref, t5_ref, t5n_ref,
                            o_ref, m_ref, l_ref, acc_ref, *, n_new, use_t5):
    p = pl.program_id(1)
    scale = HEAD_DIM ** -0.5
    rows = N_HEADS * QPAD

    @pl.when(p == 0)
    def _():
        m_ref[...] = jnp.full(m_ref.shape, NEG, jnp.float32)
        l_ref[...] = jnp.zeros(l_ref.shape, jnp.float32)
        acc_ref[...] = jnp.zeros(acc_ref.shape, jnp.float32)

    s_parts = []
    for h in range(N_HEADS):
        k_h = _bf(k_ref[:, h, :])
        s = lax.dot_general(_bf(q_ref[h]), k_h, _NT, preferred_element_type=jnp.float32) * scale
        if use_t5:
            s = s + t5_ref[h] + cb_ref[...]
        else:
            s = s + cb_ref[h:h + 1, :]
        s_parts.append(s)
    s = jnp.concatenate(s_parts, axis=0)
    m_old = m_ref[...]
    m_new = jnp.maximum(m_old, jnp.max(s, axis=-1, keepdims=True))
    alpha = jnp.exp(m_old - m_new)
    pr = jnp.exp(s - m_new)
    l_ref[...] = alpha * l_ref[...] + jnp.sum(pr, axis=-1, keepdims=True)
    pv = [jnp.dot(_bf(pr[h * QPAD:(h + 1) * QPAD]), _bf(v_ref[:, h, :]), preferred_element_type=jnp.float32)
          for h in range(N_HEADS)]
    acc_ref[...] = alpha * acc_ref[...] + jnp.concatenate(pv, axis=0)
    m_ref[...] = m_new

    @pl.when(p == pl.num_programs(1) - 1)
    def _():
        qrow = lax.broadcasted_iota(jnp.int32, (QPAD, 1), 0)
        s_new = []
        for j in range(n_new):
            cols = []
            for h in range(N_HEADS):
                sj = jnp.sum(q_ref[h] * kn_ref[h, j:j + 1, :], axis=-1, keepdims=True) * scale
                if use_t5:
                    sj = sj + t5n_ref[h, :, j:j + 1] + cbn_ref[:, j:j + 1]
                else:
                    sj = sj + cbn_ref[h:h + 1, j:j + 1]
                cols.append(jnp.where(j <= qrow, sj, NEG))
            s_new.append(jnp.concatenate(cols, axis=0))
        m_fin = m_ref[...]
        for sj in s_new:
            m_fin = jnp.maximum(m_fin, sj)
        alpha = jnp.exp(m_ref[...] - m_fin)
        l_fin = alpha * l_ref[...]
        acc = alpha * acc_ref[...]
        for j in range(n_new):
            pj = jnp.exp(s_new[j] - m_fin)
            l_fin = l_fin + pj
            vj = jnp.concatenate([jnp.broadcast_to(vn_ref[h, j:j + 1, :], (QPAD, HEAD_DIM))
                                  for h in range(N_HEADS)], axis=0)
            acc = acc + pj * vj
        o_ref[...] = acc / l_fin


def paged_attention(page_table, q, pool_k, pool_v, k_new, v_new, col_bias, t5, n_new):
    n_seq, n_pages = page_table.shape
    use_t5 = t5 is not None
    r = col_bias.shape[1]
    if not use_t5:
        t5 = jnp.zeros((N_HEADS, QPAD, (n_pages + 1) * PAGE), jnp.float32)
    qspec = pl.BlockSpec((None, N_HEADS, QPAD, HEAD_DIM), lambda b, p, pt: (b, 0, 0, 0))
    pspec = pl.BlockSpec((None, PAGE, N_HEADS, HEAD_DIM), lambda b, p, pt: (pt[b, p], 0, 0, 0))
    rows = N_HEADS * QPAD
    grid_spec = pltpu.PrefetchScalarGridSpec(
        num_scalar_prefetch=1,
        grid=(n_seq, n_pages),
        in_specs=[qspec, pspec, pspec, qspec, qspec,
                  pl.BlockSpec((None, r, PAGE), lambda b, p, pt: (b, 0, p)),
                  pl.BlockSpec((None, r, PAGE), lambda b, p, pt: (b, 0, n_pages)),
                  pl.BlockSpec((N_HEADS, QPAD, PAGE), lambda b, p, pt: (0, 0, p)),
                  pl.BlockSpec((N_HEADS, QPAD, PAGE), lambda b, p, pt: (0, 0, n_pages))],
        out_specs=pl.BlockSpec((None, rows, HEAD_DIM), lambda b, p, pt: (b, 0, 0)),
        scratch_shapes=[pltpu.VMEM((rows, 1), jnp.float32),
                        pltpu.VMEM((rows, 1), jnp.float32),
                        pltpu.VMEM((rows, HEAD_DIM), jnp.float32)],
    )
    return pl.pallas_call(
        functools.partial(_paged_attention_kernel, n_new=n_new, use_t5=use_t5),
        grid_spec=grid_spec,
        out_shape=jax.ShapeDtypeStruct((n_seq, rows, HEAD_DIM), jnp.float32),
        compiler_params=_params(("parallel", "arbitrary")),
        name="paged_attention_dsa" if use_t5 else "paged_attention_fox",
    )(page_table, q, pool_k, pool_v, k_new, v_new, col_bias, col_bias, t5, t5)


def _paged_index_score_kernel(pt_ref, qi_ref, w_ref, kpage_ref, knew_ref, o_ref):
    p = pl.program_id(1)

    def scores(keys):
        z = lax.dot_general(_bf(qi_ref[...]), _bf(keys), _NT, preferred_element_type=jnp.float32)
        r = jnp.maximum(z, 0.0) * w_ref[...]
        return jnp.sum(r.reshape(H_IDX, QPAD, PAGE), axis=0) * ((H_IDX * D_IDX) ** -0.5)

    @pl.when(p < pl.num_programs(1) - 1)
    def _():
        o_ref[...] = scores(kpage_ref[...])

    @pl.when(p == pl.num_programs(1) - 1)
    def _():
        o_ref[...] = scores(knew_ref[...])


def paged_index_scores(page_table, qi, w, pool_ki, ki_new):
    n_seq, n_pages = page_table.shape
    rows = H_IDX * QPAD
    grid_spec = pltpu.PrefetchScalarGridSpec(
        num_scalar_prefetch=1,
        grid=(n_seq, n_pages + 1),
        in_specs=[pl.BlockSpec((None, rows, D_IDX), lambda b, p, pt: (b, 0, 0)),
                  pl.BlockSpec((None, rows, 1), lambda b, p, pt: (b, 0, 0)),
                  pl.BlockSpec((None, PAGE, D_IDX),
                               lambda b, p, pt: (pt[b, jnp.minimum(p, n_pages - 1)], 0, 0)),
                  pl.BlockSpec((None, PAGE, D_IDX), lambda b, p, pt: (b, 0, 0))],
        out_specs=pl.BlockSpec((None, QPAD, PAGE), lambda b, p, pt: (b, 0, p)),
    )
    return pl.pallas_call(
        _paged_index_score_kernel,
        grid_spec=grid_spec,
        out_shape=jax.ShapeDtypeStruct((n_seq, QPAD, (n_pages + 1) * PAGE), jnp.float32),
        compiler_params=_params(("parallel", "arbitrary")),
        name="paged_index_scores",
    )(page_table, qi, w, pool_ki, ki_new)


def _paged_select_kernel(s_ref, o_ref, *, past_len, topk):
    score = s_ref[...]
    col = lax.broadcasted_iota(jnp.int32, score.shape, 2)
    qrow = lax.broadcasted_iota(jnp.int32, score.shape, 1)
    admissible = col <= past_len + qrow
    key = jnp.where(admissible, _sortable_key(score), INT_MIN)
    thr = _kth_largest_key(key, topk)
    o_ref[...] = jnp.where(admissible & (key >= thr), 0.0, NEG)


def paged_select(scores, past_len, n_new, sb):
    n_seq, _, width = scores.shape
    topk = min(TOPK_MAX, (past_len + n_new) // 4)
    spec = pl.BlockSpec((sb, QPAD, width), lambda i: (i, 0, 0))
    return pl.pallas_call(
        functools.partial(_paged_select_kernel, past_len=past_len, topk=topk),
        grid=(n_seq // sb,),
        in_specs=[spec],
        out_specs=spec,
        out_shape=jax.ShapeDtypeStruct(scores.shape, jnp.float32),
        compiler_params=_params(("parallel",)),
        name="paged_select",
    )(scores)


def _branch_kernel(oa_ref, ob_ref, ga_ref, gb_ref, wa_ref, wb_ref, o_ref):
    ya = jnp.dot(oa_ref[...], wa_ref[...], preferred_element_type=jnp.float32)
    yb = jnp.dot(ob_ref[...], wb_ref[...], preferred_element_type=jnp.float32)
    o_ref[...] = _bf(ga_ref[...] * ya + gb_ref[...] * yb)


def merge_branches(oa, ob, gates, wa, wb, tm):
    n = oa.shape[0]
    d = wa.shape[1]
    return pl.pallas_call(
        _branch_kernel,
        grid=(n // tm,),
        in_specs=[pl.BlockSpec((tm, WIDTH), lambda i: (i, 0)),
                  pl.BlockSpec((tm, WIDTH), lambda i: (i, 0)),
                  pl.BlockSpec((tm, d), lambda i: (i, 0)),
                  pl.BlockSpec((tm, d), lambda i: (i, 1)),
                  pl.BlockSpec((WIDTH, d), lambda i: (0, 0)),
                  pl.BlockSpec((WIDTH, d), lambda i: (0, 0))],
        out_specs=pl.BlockSpec((tm, d), lambda i: (i, 0)),
        out_shape=jax.ShapeDtypeStruct((n, d), jnp.bfloat16),
        compiler_params=_params(("parallel",)),
        name="merge_branches",
    )(oa, ob, gates, gates, wa, wb)


def _out_proj_kernel(u_ref, x_ref, w_ref, g_ref, x2_ref, h2_ref):
    x2 = x_ref[...] + jnp.dot(u_ref[...], w_ref[...], preferred_element_type=jnp.float32)
    x2_ref[...] = x2
    ms = jnp.mean(x2 * x2, axis=-1, keepdims=True)
    h2_ref[...] = _bf(x2 * lax.rsqrt(ms + NORM_EPS) * g_ref[...])


def out_projection(u, x, w_out, gain, tm):
    n, d = x.shape
    row = pl.BlockSpec((tm, d), lambda i: (i, 0))
    return pl.pallas_call(
        _out_proj_kernel,
        grid=(n // tm,),
        in_specs=[row, row, pl.BlockSpec((d, d), lambda i: (0, 0)), pl.BlockSpec((1, d), lambda i: (0, 0))],
        out_specs=[row, row],
        out_shape=[jax.ShapeDtypeStruct((n, d), jnp.float32), jax.ShapeDtypeStruct((n, d), jnp.bfloat16)],
        compiler_params=_params(("parallel",)),
        name="out_projection",
    )(u, x, w_out, gain.reshape(1, d))


def _staircase_pairs():
    return [(r, c) for r in range(PEER_TOPK) for c in range(PEER_TOPK) if (r + 1) * (c + 1) <= PEER_TOPK]


def _peer_route_kernel(h_ref, wq_ref, ka_ref, kb_ref, thr_ref, e0_ref, s1_ref, e1_ref, x_scr, a_scr):
    tm = h_ref.shape[0]
    q = _bf(jnp.dot(h_ref[...], wq_ref[...], preferred_element_type=jnp.float32))
    sa = lax.dot_general(ka_ref[...], q, _NT, preferred_element_type=jnp.float32)
    sb = lax.dot_general(kb_ref[...], q, _NT, preferred_element_type=jnp.float32)
    x_scr[...] = sa.reshape(2, PEER_NKEYS, PEER_HEADS, tm)

    def extract(r, carry):
        for c in range(2):
            x = x_scr[c]
            m = jnp.max(x, axis=0)
            a_scr[c, r] = m
            x_scr[c] = jnp.where(x == m[None], -jnp.inf, x)
        return carry

    lax.fori_loop(0, PEER_TOPK, extract, 0)
    a0 = [a_scr[0, r] for r in range(PEER_TOPK)]
    a1 = [a_scr[1, r] for r in range(PEER_TOPK)]
    pairs = _staircase_pairs()
    cands = jnp.stack([a0[r] + a1[c] for r, c in pairs], axis=0)
    work = cands
    kth = None
    for r in range(PEER_TOPK + 1):
        m = jnp.max(work, axis=0)
        if r == PEER_TOPK - 1:
            kth = m
        work = jnp.where(work == m[None], -jnp.inf, work)
    tau = 0.5 * (kth + m)
    mx = a0[0] + a1[0]
    z = jnp.sum(jnp.where(cands >= tau[None], jnp.exp(cands - mx[None]), 0.0), axis=0)
    sb = sb.reshape(2, PEER_HEADS, PEER_NKEYS, tm)
    inv_z = 1.0 / z
    for h in range(PEER_HEADS):
        s0 = sb[0, h]
        s1 = sb[1, h]
        thr_ref[h] = tau[h:h + 1, :] - s0
        e0_ref[h] = jnp.exp(s0 - a0[0][h:h + 1, :]) * inv_z[h:h + 1, :]
        s1_ref[h] = s1
        e1_ref[h] = jnp.exp(s1 - a1[0][h:h + 1, :])


def peer_route(h2, wq, ka, kb, tm):
    n, d = h2.shape
    out = jax.ShapeDtypeStruct((PEER_HEADS, PEER_NKEYS, n), jnp.float32)
    ospec = pl.BlockSpec((PEER_HEADS, PEER_NKEYS, tm), lambda i: (0, 0, i))
    const = lambda shape: pl.BlockSpec(shape, lambda i: (0,) * len(shape))
    return pl.pallas_call(
        _peer_route_kernel,
        grid=(n // tm,),
        in_specs=[pl.BlockSpec((tm, d), lambda i: (i, 0)), const(wq.shape), const(ka.shape), const(kb.shape)],
        out_specs=[ospec] * 4,
        out_shape=[out] * 4,
        scratch_shapes=[pltpu.VMEM((2, PEER_NKEYS, PEER_HEADS, tm), jnp.float32),
                        pltpu.VMEM((2, PEER_TOPK, PEER_HEADS, tm), jnp.float32)],
        compiler_params=_params(("parallel",)),
        name="peer_route",
    )(h2, wq, ka, kb)


def _peer_expert_kernel(h_ref, x_ref, u_ref, vt_ref, thr_ref, e0_ref, s1_ref, e1_ref, o_ref, acc_ref, g_scr, *, te):
    e = pl.program_id(1)
    n_i = te // PEER_NKEYS
    blocks = thr_ref.shape[1] // n_i
    base = (e % blocks) * n_i

    @pl.when(e == 0)
    def _():
        acc_ref[...] = jnp.zeros(acc_ref.shape, jnp.float32)

    for ii in range(n_i):
        g = None
        for h in range(PEER_HEADS):
            thr = thr_ref[h, pl.ds(base + ii, 1), :]
            e0 = e0_ref[h, pl.ds(base + ii, 1), :]
            term = jnp.where(s1_ref[h] >= thr, e1_ref[h], 0.0) * e0
            g = term if g is None else g + term
        g_scr[ii * PEER_NKEYS:(ii + 1) * PEER_NKEYS, :] = g
    ht = lax.dot_general(u_ref[...], h_ref[...], _NT, preferred_element_type=jnp.float32)
    act = 0.5 * ht * (1.0 + lax.erf(ht * (2.0 ** -0.5)))
    wt = _bf(act * g_scr[...])
    acc_ref[...] += jnp.dot(vt_ref[...], wt, preferred_element_type=jnp.float32)

    @pl.when(e == pl.num_programs(1) - 1)
    def _():
        o_ref[...] = x_ref[...] + acc_ref[...].T


def peer_experts(h2, x2, u, vt, thr, e0, s1, e1, tm, te):
    n, d = h2.shape
    n_exp = u.shape[0]
    n_i = te // PEER_NKEYS
    ib = max(8, n_i)
    route_i = pl.BlockSpec((PEER_HEADS, ib, tm), lambda i, e: (0, (e * n_i) // ib, i))
    route_j = pl.BlockSpec((PEER_HEADS, PEER_NKEYS, tm), lambda i, e: (0, 0, i))
    row = pl.BlockSpec((tm, d), lambda i, e: (i, 0))
    return pl.pallas_call(
        functools.partial(_peer_expert_kernel, te=te),
        grid=(n // tm, n_exp // te),
        in_specs=[row, row,
                  pl.BlockSpec((te, d), lambda i, e: (e, 0)),
                  pl.BlockSpec((d, te), lambda i, e: (0, e)),
                  route_i, route_i, route_j, route_j],
        out_specs=row,
        out_shape=jax.ShapeDtypeStruct((n, d), jnp.float32),
        scratch_shapes=[pltpu.VMEM((d, tm), jnp.float32), pltpu.VMEM((te, tm), jnp.float32)],
        compiler_params=_params(("parallel", "arbitrary")),
        name="peer_experts",
    )(h2, x2, u, vt, thr, e0, s1, e1)


def _t5_bucket(rel):
    n = jnp.maximum(rel, 0)
    max_exact = N_BUCKETS // 2
    nf = jnp.maximum(n, 1).astype(jnp.float32)
    large = max_exact + (jnp.log(nf / max_exact) / math.log(MAX_DISTANCE / max_exact)
                         * (N_BUCKETS - max_exact)).astype(jnp.int32)
    large = jnp.minimum(large, N_BUCKETS - 1)
    return jnp.where(n < max_exact, n, large)


def _tile(n, pref):
    t = min(n, pref)
    assert n % t == 0, (n, pref)
    return t


def _group(x, weights, batch, seq):
    n, d = x.shape
    tm = _tile(n, 512)
    h = rmsnorm_bf16(x, weights["norm_mix"], tm)
    qa, ka, ka16, va, va16 = project_qkv(h, weights["w_a"], weights["gain_a"], tm)
    qb, kb, kb16, vb, vb16 = project_qkv(h, weights["w_b"], weights["gain_b"], tm)
    zero = jnp.zeros((1, WIDTH), jnp.float32)
    qi = project(h, weights["w_qi"], zero, tm, WIDTH, "plain", jnp.bfloat16)
    small = project(h, weights["w_small"], weights["b_small"], tm, SMALL_W, "small", jnp.float32)
    gates = project(h, weights["w_gate"], weights["b_gate"], tm, WIDTH, "sigmoid", jnp.float32)
    return dict(qa=qa, ka=ka, ka16=ka16, va=va, va16=va16, qb=qb, kb=kb, kb16=kb16, vb=vb, vb16=vb16,
                qi=qi, small=small, gates=gates)


def _finish(x, oa, ob, gates, weights):
    n = x.shape[0]
    u = merge_branches(oa, ob, gates, weights["w_branch_a"], weights["w_branch_b"], _tile(n, 256))
    x2, h2 = out_projection(u, x, weights["w_out"], weights["norm_ffn"], _tile(n, 256))
    thr, e0, s1, e1 = peer_route(h2, weights["peer_wq"], weights["peer_ka"], weights["peer_kb"], _tile(n, 256))
    return peer_experts(h2, x2, weights["peer_u"], weights["peer_vt"], thr, e0, s1, e1, _tile(n, 512), 512)


def _heads_first(a, n_seq, n_new):
    a = a.astype(jnp.float32).reshape(n_seq, n_new, N_HEADS, HEAD_DIM).transpose(0, 2, 1, 3)
    return jnp.pad(a, ((0, 0), (0, 0), (0, QPAD - n_new), (0, 0)))


def _heads_last(o, n_seq, n_new):
    o = o.reshape(n_seq, N_HEADS, QPAD, HEAD_DIM)[:, :, :n_new].transpose(0, 2, 1, 3)
    return _bf(o.reshape(n_seq * n_new, WIDTH))


def kernel(x_prompt, x_sample, cache_fox_k, cache_fox_v, cache_fox_logf, cache_dsa_k, cache_dsa_v, cache_idx_k,
           page_table, norm_mix, w_in, b_forget, b_gate, qk_gain, w_branch_a, w_branch_b, w_out, rel_bias,
           norm_ffn, peer_w_query, peer_sub_keys, peer_u, peer_v):
    bp, tp, d = x_prompt.shape
    bs, ts, _ = x_sample.shape
    n_pages = page_table.shape[1]
    past_len = n_pages * PAGE
    depth = w_in.shape[0]
    assert depth == 1, "one layer per call"
    l = 0

    offs = np.cumsum([0, WIDTH, WIDTH, WIDTH, N_HEADS, WIDTH, WIDTH, WIDTH, H_IDX * D_IDX, D_IDX, H_IDX, d, d])
    w = w_in[l]
    col = lambda a, b: w[:, int(offs[a]):int(offs[b])]
    pad = SMALL_W - (D_IDX + H_IDX + N_HEADS)
    gains = jnp.tile(qk_gain[l], (1, N_HEADS))
    weights = dict(
        norm_mix=norm_mix[l], norm_ffn=norm_ffn[l],
        w_a=_bf(col(0, 3)), w_b=_bf(col(4, 7)), gain_a=gains[0:2], gain_b=gains[2:4],
        w_qi=_bf(col(7, 8)),
        w_small=_bf(jnp.concatenate([col(8, 9), col(9, 10), col(3, 4), jnp.zeros((d, pad), w.dtype)], axis=1)),
        b_small=jnp.concatenate([jnp.zeros((D_IDX + H_IDX,), jnp.float32), b_forget[l],
                                 jnp.zeros((pad,), jnp.float32)]).reshape(1, SMALL_W),
        w_gate=_bf(col(10, 12)), b_gate=b_gate[l].reshape(1, 2 * d),
        w_branch_a=_bf(w_branch_a[l]), w_branch_b=_bf(w_branch_b[l]), w_out=_bf(w_out[l]),
        peer_wq=_bf(peer_w_query[l]), peer_u=_bf(peer_u[l]), peer_vt=_bf(peer_v[l].T),
    )
    sk = peer_sub_keys[l]
    eye_h = jnp.eye(PEER_HEADS, dtype=sk.dtype)
    eye_c = jnp.eye(2, dtype=sk.dtype)
    n_rows = 2 * PEER_NKEYS * PEER_HEADS
    weights["peer_ka"] = _bf(jnp.einsum('hckd,hg,ce->ckhged', sk, eye_h, eye_c).reshape(n_rows, -1))
    weights["peer_kb"] = _bf(jnp.einsum('hckd,hg,ce->chkged', sk, eye_h, eye_c).reshape(n_rows, -1))

    def t5_of(rel):
        return jnp.moveaxis(rel_bias[_t5_bucket(rel)].astype(jnp.float32), -1, 0)

    xp = x_prompt.reshape(bp * tp, d)
    gp = _group(xp, weights, bp, tp)
    logf_p = gp["small"][:, FA_OFF:FA_OFF + N_HEADS].reshape(bp, tp, N_HEADS)
    ki_p = gp["small"][:, KI_OFF:KI_OFF + D_IDX].reshape(bp, tp, D_IDX)
    negc_p = -jnp.swapaxes(jnp.cumsum(logf_p, axis=1), 1, 2)
    ta = _tile(tp, 256)
    oa_p = fox_prompt_attention(gp["qa"], gp["ka16"], gp["va16"], negc_p, bp, tp, ta)
    mask_p = dsa_prompt_select(gp["qi"], gp["small"], bp, tp, ta)
    ar = jnp.arange(ta, dtype=jnp.int32)
    rel_tiles = jnp.stack([ar[:, None] - ar[None, :], ta + ar[:, None] - ar[None, :],
                           jnp.full((ta, ta), 2 * ta, jnp.int32)])
    t5_tiles = t5_of(rel_tiles).transpose(1, 0, 2, 3)
    ob_p = dsa_prompt_attention(gp["qb"], gp["kb16"], gp["vb16"], mask_p, t5_tiles, bp, tp, ta)
    y_prompt = _finish(xp, oa_p, ob_p, gp["gates"], weights).reshape(bp, tp, d)

    xs = x_sample.reshape(bs * ts, d)
    gs = _group(xs, weights, bs, ts)
    logf_s = gs["small"][:, FA_OFF:FA_OFF + N_HEADS].reshape(bs, ts, N_HEADS)
    ki_s = gs["small"][:, KI_OFF:KI_OFF + D_IDX].reshape(bs, ts, D_IDX)
    wi_s = gs["small"][:, WI_OFF:WI_OFF + H_IDX].reshape(bs, ts, H_IDX)
    width = (n_pages + 1) * PAGE
    logf_past = cache_fox_logf[l][page_table].reshape(bs, past_len, N_HEADS).astype(jnp.float32)
    c_all = jnp.cumsum(jnp.concatenate([logf_past, logf_s], axis=1), axis=1)
    negc_s = jnp.pad(-jnp.swapaxes(c_all, 1, 2), ((0, 0), (0, 0), (0, width - past_len - ts)))
    oa_s = paged_attention(page_table, _heads_first(gs["qa"], bs, ts), cache_fox_k[l], cache_fox_v[l],
                           _heads_first(gs["ka"], bs, ts), _heads_first(gs["va"], bs, ts), negc_s, None, ts)
    qi_s = gs["qi"].astype(jnp.float32).reshape(bs, ts, H_IDX, D_IDX).transpose(0, 2, 1, 3)
    qi_s = jnp.pad(qi_s, ((0, 0), (0, 0), (0, QPAD - ts), (0, 0))).reshape(bs, H_IDX * QPAD, D_IDX)
    w_s = jnp.pad(wi_s.transpose(0, 2, 1), ((0, 0), (0, 0), (0, QPAD - ts))).reshape(bs, H_IDX * QPAD, 1)
    ki_new = jnp.pad(ki_s, ((0, 0), (0, PAGE - ts), (0, 0)))
    scores_s = paged_index_scores(page_table, qi_s, w_s, cache_idx_k[l], ki_new)
    mask_s = paged_select(scores_s, past_len, ts, _tile(bs, 16))
    rel_s = past_len + jnp.arange(QPAD, dtype=jnp.int32)[:, None] - jnp.arange(width, dtype=jnp.int32)[None, :]
    ob_s = paged_attention(page_table, _heads_first(gs["qb"], bs, ts), cache_dsa_k[l], cache_dsa_v[l],
                           _heads_first(gs["kb"], bs, ts), _heads_first(gs["vb"], bs, ts), mask_s, t5_of(rel_s), ts)
    y_sample = _finish(xs, _heads_last(oa_s, bs, ts), _heads_last(ob_s, bs, ts), gs["gates"],
                       weights).reshape(bs, ts, d)

    def rows(g, b, t):
        kv = lambda a: a.reshape(1, b, t, N_HEADS, HEAD_DIM)
        return (kv(g["ka"]), kv(g["va"]), None, kv(g["kb"]), kv(g["vb"]), None)

    p_rows = rows(gp, bp, tp)
    s_rows = rows(gs, bs, ts)
    return (y_prompt, y_sample,
            p_rows[0], p_rows[1], logf_p[None], p_rows[3], p_rows[4], ki_p[None],
            s_rows[0], s_rows[1], logf_s[None], s_rows[3], s_rows[4], ki_s[None])
```

```python
import functools
import math

import jax
import jax.numpy as jnp
import numpy as np
from jax import lax
from jax.experimental import pallas as pl
from jax.experimental.pallas import tpu as pltpu

HEAD_DIM = 128
N_HEADS = 8
WIDTH = N_HEADS * HEAD_DIM
H_IDX = 16
D_IDX = 64
TOPK_MAX = 256
N_BUCKETS = 32
MAX_DISTANCE = 128
PAGE = 128
PEER_HEADS = 8
PEER_NKEYS = 128
PEER_HALF = 64
PEER_TOPK = 16
NORM_EPS = 1e-6
NEG = -1e30
INT_MIN = -2 ** 31
SMALL_W = 128
KI_OFF, WI_OFF, FA_OFF = 0, 64, 80

VMEM_LIMIT = 56 * 1024 * 1024

_NT = (((1,), (1,)), ((), ()))


def _params(sem):
    return pltpu.CompilerParams(dimension_semantics=sem, vmem_limit_bytes=VMEM_LIMIT)


def _bf(x):
    return x.astype(jnp.bfloat16)


def _rmsnorm_kernel(x_ref, g_ref, o_ref):
    x = x_ref[...]
    ms = jnp.mean(x * x, axis=-1, keepdims=True)
    o_ref[...] = (x * lax.rsqrt(ms + NORM_EPS) * g_ref[...]).astype(o_ref.dtype)


def rmsnorm_bf16(x, gain, tm):
    n, d = x.shape
    return pl.pallas_call(
        _rmsnorm_kernel,
        grid=(n // tm,),
        in_specs=[pl.BlockSpec((tm, d), lambda i: (i, 0)),
                  pl.BlockSpec((1, d), lambda i: (0, 0))],
        out_specs=pl.BlockSpec((tm, d), lambda i: (i, 0)),
        out_shape=jax.ShapeDtypeStruct((n, d), jnp.bfloat16),
        compiler_params=_params(("parallel",)),
        name="rmsnorm",
    )(x, gain.reshape(1, d))


def _head_rmsnorm(y, gain):
    outs = []
    for h in range(N_HEADS):
        yh = y[:, h * HEAD_DIM:(h + 1) * HEAD_DIM]
        ms = jnp.mean(yh * yh, axis=-1, keepdims=True)
        outs.append(yh * lax.rsqrt(ms + NORM_EPS) * gain[:, h * HEAD_DIM:(h + 1) * HEAD_DIM])
    return jnp.concatenate(outs, axis=-1)


def _qkv_kernel(h_ref, w_ref, g_ref, q_ref, k32_ref, k16_ref, v32_ref, v16_ref):
    j = pl.program_id(1)
    y = jnp.dot(h_ref[...], w_ref[...], preferred_element_type=jnp.float32)

    @pl.when(j == 0)
    def _():
        q_ref[...] = _bf(_head_rmsnorm(y, g_ref[0:1, :]))

    @pl.when(j == 1)
    def _():
        k = _head_rmsnorm(y, g_ref[1:2, :])
        k32_ref[...] = k
        k16_ref[...] = _bf(k)

    @pl.when(j == 2)
    def _():
        v32_ref[...] = y
        v16_ref[...] = _bf(y)


def project_qkv(h, w, gains, tm):
    n, d = h.shape
    row = pl.BlockSpec((tm, WIDTH), lambda i, j: (i, 0))
    f32 = jax.ShapeDtypeStruct((n, WIDTH), jnp.float32)
    b16 = jax.ShapeDtypeStruct((n, WIDTH), jnp.bfloat16)
    return pl.pallas_call(
        _qkv_kernel,
        grid=(n // tm, 3),
        in_specs=[pl.BlockSpec((tm, d), lambda i, j: (i, 0)),
                  pl.BlockSpec((d, WIDTH), lambda i, j: (0, j)),
                  pl.BlockSpec((2, WIDTH), lambda i, j: (0, 0))],
        out_specs=[row, row, row, row, row],
        out_shape=[b16, f32, b16, f32, b16],
        compiler_params=_params(("parallel", "arbitrary")),
        name="project_qkv",
    )(h, w, gains)


def _log_sigmoid(x):
    return jnp.minimum(x, 0.0) - jnp.log(1.0 + jnp.exp(-jnp.abs(x)))


def _proj_kernel(h_ref, w_ref, b_ref, o_ref, *, mode):
    y = jnp.dot(h_ref[...], w_ref[...], preferred_element_type=jnp.float32)
    if mode == "sigmoid":
        y = jax.nn.sigmoid(y + b_ref[...])
    elif mode == "small":
        lane = lax.broadcasted_iota(jnp.int32, y.shape, 1)
        is_f = (lane >= FA_OFF) & (lane < FA_OFF + N_HEADS)
        y = jnp.where(is_f, _log_sigmoid(y + b_ref[...]), y)
    o_ref[...] = y.astype(o_ref.dtype)


def project(h, w, bias, tm, tn, mode, out_dtype):
    n, d = h.shape
    nc = w.shape[1]
    return pl.pallas_call(
        functools.partial(_proj_kernel, mode=mode),
        grid=(n // tm, nc // tn),
        in_specs=[pl.BlockSpec((tm, d), lambda i, j: (i, 0)),
                  pl.BlockSpec((d, tn), lambda i, j: (0, j)),
                  pl.BlockSpec((1, tn), lambda i, j: (0, j))],
        out_specs=pl.BlockSpec((tm, tn), lambda i, j: (i, j)),
        out_shape=jax.ShapeDtypeStruct((n, nc), out_dtype),
        compiler_params=_params(("parallel", "arbitrary")),
        name="project_" + mode,
    )(h, w, bias)


def _online_softmax_step(s, h, m_ref, l_ref, acc_ref, v_h):
    m_old = m_ref[h]
    m_new = jnp.maximum(m_old, jnp.max(s, axis=-1, keepdims=True))
    alpha = jnp.exp(m_old - m_new)
    p = jnp.exp(s - m_new)
    l_ref[h] = alpha * l_ref[h] + jnp.sum(p, axis=-1, keepdims=True)
    acc_ref[h] = alpha * acc_ref[h] + jnp.dot(_bf(p), v_h, preferred_element_type=jnp.float32)
    m_ref[h] = m_new


def _flash_init(m_ref, l_ref, acc_ref):
    m_ref[...] = jnp.full(m_ref.shape, NEG, jnp.float32)
    l_ref[...] = jnp.zeros(l_ref.shape, jnp.float32)
    acc_ref[...] = jnp.zeros(acc_ref.shape, jnp.float32)


def _flash_finish(o_ref, l_ref, acc_ref):
    for h in range(N_HEADS):
        o_ref[:, h * HEAD_DIM:(h + 1) * HEAD_DIM] = (acc_ref[h] / l_ref[h]).astype(o_ref.dtype)


def _fox_prompt_kernel(q_ref, k_ref, v_ref, negc_ref, o_ref, m_ref, l_ref, acc_ref, *, tq, tk):
    qi, ki = pl.program_id(1), pl.program_id(2)

    @pl.when(ki == 0)
    def _():
        _flash_init(m_ref, l_ref, acc_ref)

    @pl.when(ki <= qi)
    def _():
        row = qi * tq + lax.broadcasted_iota(jnp.int32, (tq, tk), 0)
        col = ki * tk + lax.broadcasted_iota(jnp.int32, (tq, tk), 1)
        causal = col <= row
        scale = HEAD_DIM ** -0.5
        for h in range(N_HEADS):
            sl = slice(h * HEAD_DIM, (h + 1) * HEAD_DIM)
            s = lax.dot_general(q_ref[:, sl], k_ref[:, sl], _NT, preferred_element_type=jnp.float32)
            s = jnp.where(causal, s * scale + negc_ref[h:h + 1, :], NEG)
            _online_softmax_step(s, h, m_ref, l_ref, acc_ref, v_ref[:, sl])

    @pl.when(ki == pl.num_programs(2) - 1)
    def _():
        _flash_finish(o_ref, l_ref, acc_ref)


def _dsa_prompt_kernel(q_ref, k_ref, v_ref, mask_ref, t5_ref, o_ref, m_ref, l_ref, acc_ref):
    qi, ki = pl.program_id(1), pl.program_id(2)

    @pl.when(ki == 0)
    def _():
        _flash_init(m_ref, l_ref, acc_ref)

    @pl.when(ki <= qi)
    def _():
        off = jnp.minimum(qi - ki, 2)
        mask = mask_ref[...].astype(jnp.float32)
        scale = HEAD_DIM ** -0.5
        for h in range(N_HEADS):
            sl = slice(h * HEAD_DIM, (h + 1) * HEAD_DIM)
            s = lax.dot_general(q_ref[:, sl], k_ref[:, sl], _NT, preferred_element_type=jnp.float32)
            s = s * scale + t5_ref[off, h] + mask
            _online_softmax_step(s, h, m_ref, l_ref, acc_ref, v_ref[:, sl])

    @pl.when(ki == pl.num_programs(2) - 1)
    def _():
        _flash_finish(o_ref, l_ref, acc_ref)


def _flash_scratch(tq):
    return [pltpu.VMEM((N_HEADS, tq, 1), jnp.float32),
            pltpu.VMEM((N_HEADS, tq, 1), jnp.float32),
            pltpu.VMEM((N_HEADS, tq, HEAD_DIM), jnp.float32)]


def fox_prompt_attention(q, k, v, negc, batch, seq, t):
    nb = seq // t
    qspec = pl.BlockSpec((t, WIDTH), lambda b, i, j: (b * nb + i, 0))
    kspec = pl.BlockSpec((t, WIDTH), lambda b, i, j: (b * nb + jnp.minimum(i, j), 0))
    return pl.pallas_call(
        functools.partial(_fox_prompt_kernel, tq=t, tk=t),
        grid=(batch, nb, nb),
        in_specs=[qspec, kspec, kspec,
                  pl.BlockSpec((None, N_HEADS, t), lambda b, i, j: (b, 0, jnp.minimum(i, j)))],
        out_specs=qspec,
        out_shape=jax.ShapeDtypeStruct((batch * seq, WIDTH), jnp.bfloat16),
        scratch_shapes=_flash_scratch(t),
        compiler_params=_params(("parallel", "parallel", "arbitrary")),
        name="fox_prompt",
    )(q, k, v, negc)


def dsa_prompt_attention(q, k, v, mask, t5, batch, seq, t):
    nb = seq // t
    qspec = pl.BlockSpec((t, WIDTH), lambda b, i, j: (b * nb + i, 0))
    kspec = pl.BlockSpec((t, WIDTH), lambda b, i, j: (b * nb + jnp.minimum(i, j), 0))
    return pl.pallas_call(
        _dsa_prompt_kernel,
        grid=(batch, nb, nb),
        in_specs=[qspec, kspec, kspec,
                  pl.BlockSpec((t, t), lambda b, i, j: (b * nb + i, jnp.minimum(i, j))),
                  pl.BlockSpec((3, N_HEADS, t, t), lambda b, i, j: (0, 0, 0, 0))],
        out_specs=qspec,
        out_shape=jax.ShapeDtypeStruct((batch * seq, WIDTH), jnp.bfloat16),
        scratch_shapes=_flash_scratch(t),
        compiler_params=_params(("parallel", "parallel", "arbitrary")),
        name="dsa_prompt",
    )(q, k, v, mask, t5)


def _sortable_key(score):
    bits = pltpu.bitcast(score, jnp.int32)
    return jnp.where(bits < 0, bits ^ jnp.int32(0x7FFFFFFF), bits)


def _kth_largest_key(key, k):
    kf = jnp.float32(k)

    def count_ge(cand):
        return jnp.sum(jnp.where(key >= cand, 1.0, 0.0), axis=-1, keepdims=True)

    t = jnp.full(key.shape[:-1] + (1,), INT_MIN, jnp.int32)
    cand = jnp.zeros_like(t)
    t = jnp.where(count_ge(cand) >= kf, cand, t)

    def body(it, t):
        cand = t | jnp.left_shift(jnp.int32(1), 30 - it)
        return jnp.where(count_ge(cand) >= kf, cand, t)

    return lax.fori_loop(0, 31, body, t)


def _dsa_prompt_select_kernel(qi_ref, wq_ref, kk_ref, o_ref, *, tq, topk):
    i = pl.program_id(1)
    seq = kk_ref.shape[0]
    ki = _bf(kk_ref[:, KI_OFF:KI_OFF + D_IDX])
    wi = wq_ref[:, WI_OFF:WI_OFF + H_IDX]
    score = jnp.zeros((tq, seq), jnp.float32)
    for h in range(H_IDX):
        z = lax.dot_general(qi_ref[:, h * D_IDX:(h + 1) * D_IDX], ki, _NT,
                            preferred_element_type=jnp.float32)
        score = score + jnp.maximum(z, 0.0) * wi[:, h:h + 1]
    score = score * ((H_IDX * D_IDX) ** -0.5)
    row = i * tq + lax.broadcasted_iota(jnp.int32, (tq, seq), 0)
    col = lax.broadcasted_iota(jnp.int32, (tq, seq), 1)
    admissible = col <= row
    key = jnp.where(admissible, _sortable_key(score), INT_MIN)
    thr = _kth_largest_key(key, topk)
    o_ref[...] = jnp.where(admissible & (key >= thr), 0.0, NEG).astype(o_ref.dtype)


def dsa_prompt_select(qi, small, batch, seq, tq):
    nb = seq // tq
    topk = min(TOPK_MAX, seq // 4)
    return pl.pallas_call(
        functools.partial(_dsa_prompt_select_kernel, tq=tq, topk=topk),
        grid=(batch, nb),
        in_specs=[pl.BlockSpec((tq, H_IDX * D_IDX), lambda b, i: (b * nb + i, 0)),
                  pl.BlockSpec((tq, SMALL_W), lambda b, i: (b * nb + i, 0)),
                  pl.BlockSpec((seq, SMALL_W), lambda b, i: (b, 0))],
        out_specs=pl.BlockSpec((tq, seq), lambda b, i: (b * nb + i, 0)),
        out_shape=jax.ShapeDtypeStruct((batch * seq, seq), jnp.bfloat16),
        compiler_params=_params(("parallel", "arbitrary")),
        name="dsa_prompt_select",
    )(qi, small, small)


QPAD = 8
NEWP = 16
ROWS = N_HEADS * QPAD
PAGE_COLS = PAGE * N_HEADS
NEW_COLS = NEWP * N_HEADS


def _attend(q, blocks, m_ref, l_ref, acc_ref):
    scale = HEAD_DIM ** -0.5
    scores = [lax.dot_general(q, k, _NT, preferred_element_type=jnp.float32) * scale + b for k, _, b in blocks]
    top = scores[0]
    for s in scores[1:]:
        top = jnp.maximum(top, s)
    m_old = m_ref[...]
    m_new = jnp.maximum(m_old, jnp.max(top, axis=-1, keepdims=True))
    alpha = jnp.exp(m_old - m_new)
    acc = alpha * acc_ref[...]
    total = None
    for s, (_, v, _) in zip(scores, blocks):
        p = jnp.exp(s - m_new)
        total = p if total is None else total + p
        acc = acc + jnp.dot(_bf(p), v, preferred_element_type=jnp.float32)
    l_ref[...] = alpha * l_ref[...] + jnp.sum(total, axis=-1, keepdims=True)
    acc_ref[...] = acc
    m_ref[...] = m_new


def _paged_attention_kernel(pt_ref, q_ref, *refs, ps, n_pages, dsa):
    k_refs, v_refs = refs[:ps], refs[ps:2 * ps]
    kn_ref, vn_ref, diag_ref, newmask_ref = refs[2 * ps:2 * ps + 4]
    rest = refs[2 * ps + 4:]
    if dsa:
        keep_ref, keepn_ref, expand_ref, t5far_ref, t5last_ref, t5new_ref = rest[:6]
        rest = rest[6:]
    else:
        cb_ref, cbn_ref = rest[:2]
        rest = rest[2:]
    o_ref, m_ref, l_ref, acc_ref = rest
    step = pl.program_id(1)

    @pl.when(step == 0)
    def _():
        m_ref[...] = jnp.full(m_ref.shape, NEG, jnp.float32)
        l_ref[...] = jnp.zeros(l_ref.shape, jnp.float32)
        acc_ref[...] = jnp.zeros(acc_ref.shape, jnp.float32)

    def keep_bias(keep, cols):
        spread = jnp.dot(_bf(keep), expand_ref[:, :cols], preferred_element_type=jnp.float32)
        return jnp.tile((spread - 1.0) * (-NEG), (N_HEADS, 1))

    q = _bf(q_ref[...])
    blocks = []
    for j in range(ps):
        kmat = _bf(k_refs[j][...].reshape(PAGE_COLS, HEAD_DIM))
        vmat = _bf(v_refs[j][...].reshape(PAGE_COLS, HEAD_DIM))
        if dsa:
            t5 = jnp.where(step * ps + j == n_pages - 1, t5last_ref[...], t5far_ref[...])
            bias = diag_ref[...] + t5 + keep_bias(keep_ref[:, j * PAGE:(j + 1) * PAGE], PAGE_COLS)
        else:
            bias = diag_ref[...] + cb_ref[:, j * PAGE_COLS:(j + 1) * PAGE_COLS]
        blocks.append((kmat, vmat, bias))
    _attend(q, blocks, m_ref, l_ref, acc_ref)

    @pl.when(step == pl.num_programs(1) - 1)
    def _():
        kmat = _bf(kn_ref[...].reshape(NEW_COLS, HEAD_DIM))
        vmat = _bf(vn_ref[...].reshape(NEW_COLS, HEAD_DIM))
        if dsa:
            bias = newmask_ref[...] + t5new_ref[...] + keep_bias(keepn_ref[...], NEW_COLS)
        else:
            bias = newmask_ref[...] + cbn_ref[...]
        _attend(q, [(kmat, vmat, bias)], m_ref, l_ref, acc_ref)
        o_ref[...] = acc_ref[...] / l_ref[...]


def _head_masks(n_new):
    row_h = np.arange(ROWS)[:, None] // QPAD
    row_q = np.arange(ROWS)[:, None] % QPAD
    col = np.arange(PAGE_COLS)[None, :]
    diag = np.where(col % N_HEADS == row_h, 0.0, NEG).astype(np.float32)
    coln = np.arange(NEW_COLS)[None, :]
    pos = coln // N_HEADS
    ok = (coln % N_HEADS == row_h) & (pos <= row_q) & (pos < n_new)
    return jnp.asarray(diag), jnp.asarray(np.where(ok, 0.0, NEG).astype(np.float32))


def paged_attention(page_table, q, pool_k, pool_v, k_new, v_new, n_new, *, col_bias=None, keep=None, t5=None):
    n_seq, n_pages = page_table.shape
    dsa = keep is not None
    ps = math.gcd(8, n_pages)
    diag, newmask = _head_masks(n_new)
    const = lambda a: pl.BlockSpec(a.shape, lambda b, p, pt: (0,) * a.ndim)
    qspec = pl.BlockSpec((None, ROWS, HEAD_DIM), lambda b, p, pt: (b, 0, 0))
    nspec = pl.BlockSpec((None, NEWP, N_HEADS, HEAD_DIM), lambda b, p, pt: (b, 0, 0, 0))
    pspecs = [pl.BlockSpec((None, PAGE, N_HEADS, HEAD_DIM), lambda b, p, pt, j=j: (pt[b, p * ps + j], 0, 0, 0))
              for j in range(ps)]
    in_specs = [qspec] + pspecs + pspecs + [nspec, nspec, const(diag), const(newmask)]
    args = [q] + [pool_k] * ps + [pool_v] * ps + [k_new, v_new, diag, newmask]
    if dsa:
        pos = np.arange(PAGE)[:, None]
        expand = jnp.asarray((np.arange(PAGE_COLS)[None, :] // N_HEADS == pos), jnp.bfloat16)
        in_specs += [pl.BlockSpec((None, QPAD, ps * PAGE), lambda b, p, pt: (b, 0, p)),
                     pl.BlockSpec((None, QPAD, PAGE), lambda b, p, pt: (b, 0, n_pages)),
                     const(expand)] + [const(a) for a in t5]
        args += [keep, keep, expand] + list(t5)
    else:
        in_specs += [pl.BlockSpec((None, 1, ps * PAGE_COLS), lambda b, p, pt: (b, 0, p)),
                     pl.BlockSpec((None, 1, NEW_COLS), lambda b, p, pt: (b, 0, n_pages * N_HEADS))]
        args += [col_bias, col_bias]
    grid_spec = pltpu.PrefetchScalarGridSpec(
        num_scalar_prefetch=1,
        grid=(n_seq, n_pages // ps),
        in_specs=in_specs,
        out_specs=pl.BlockSpec((None, ROWS, HEAD_DIM), lambda b, p, pt: (b, 0, 0)),
        scratch_shapes=[pltpu.VMEM((ROWS, 1), jnp.float32),
                        pltpu.VMEM((ROWS, 1), jnp.float32),
                        pltpu.VMEM((ROWS, HEAD_DIM), jnp.float32)],
    )
    return pl.pallas_call(
        functools.partial(_paged_attention_kernel, ps=ps, n_pages=n_pages, dsa=dsa),
        grid_spec=grid_spec,
        out_shape=jax.ShapeDtypeStruct((n_seq, ROWS, HEAD_DIM), jnp.float32),
        compiler_params=_params(("parallel", "arbitrary")),
        name="paged_attention_dsa" if dsa else "paged_attention_fox",
    )(page_table, *args)


def _paged_index_score_kernel(pt_ref, qi_ref, w_ref, *refs, ps):
    kpage_refs = refs[:ps]
    knew_ref, o_ref, on_ref = refs[ps:]
    qi = _bf(qi_ref[...])

    def scores(keys):
        z = lax.dot_general(qi, _bf(keys), _NT, preferred_element_type=jnp.float32)
        r = jnp.maximum(z, 0.0) * w_ref[...]
        return jnp.sum(r.reshape(H_IDX, QPAD, PAGE), axis=0) * ((H_IDX * D_IDX) ** -0.5)

    for j in range(ps):
        o_ref[:, j * PAGE:(j + 1) * PAGE] = scores(kpage_refs[j][...])

    @pl.when(pl.program_id(1) == pl.num_programs(1) - 1)
    def _():
        on_ref[...] = scores(knew_ref[...])


def paged_index_scores(page_table, qi, w, pool_ki, ki_new):
    n_seq, n_pages = page_table.shape
    ps = math.gcd(8, n_pages)
    rows = H_IDX * QPAD
    pspecs = [pl.BlockSpec((None, PAGE, D_IDX), lambda b, p, pt, j=j: (pt[b, p * ps + j], 0, 0)) for j in range(ps)]
    grid_spec = pltpu.PrefetchScalarGridSpec(
        num_scalar_prefetch=1,
        grid=(n_seq, n_pages // ps),
        in_specs=[pl.BlockSpec((None, rows, D_IDX), lambda b, p, pt: (b, 0, 0)),
                  pl.BlockSpec((None, rows, 1), lambda b, p, pt: (b, 0, 0))] + pspecs +
                 [pl.BlockSpec((None, PAGE, D_IDX), lambda b, p, pt: (b, 0, 0))],
        out_specs=[pl.BlockSpec((None, QPAD, ps * PAGE), lambda b, p, pt: (b, 0, p)),
                   pl.BlockSpec((None, QPAD, PAGE), lambda b, p, pt: (b, 0, 0))],
    )
    return pl.pallas_call(
        functools.partial(_paged_index_score_kernel, ps=ps),
        grid_spec=grid_spec,
        out_shape=[jax.ShapeDtypeStruct((n_seq, QPAD, n_pages * PAGE), jnp.float32),
                   jax.ShapeDtypeStruct((n_seq, QPAD, PAGE), jnp.float32)],
        compiler_params=_params(("parallel", "arbitrary")),
        name="paged_index_scores",
    )(page_table, qi, w, *([pool_ki] * ps), ki_new)


def _strided_cumsum(x, stride):
    n = x.shape[-1]
    lane = lax.broadcasted_iota(jnp.int32, x.shape, x.ndim - 1)
    sh = stride
    while sh < n:
        x = x + jnp.where(lane >= sh, pltpu.roll(x, sh, axis=x.ndim - 1), 0.0)
        sh *= 2
    return x


def _paged_logf_kernel(pt_ref, *refs, n_pages):
    page_refs = refs[:n_pages]
    new_ref, o_ref, x_scr = refs[n_pages:]
    for p in range(n_pages):
        x_scr[p:p + 1, :] = page_refs[p][...]
    x = _strided_cumsum(x_scr[...], N_HEADS)
    lane = lax.broadcasted_iota(jnp.int32, x.shape, 1)
    tot = jnp.where(lane >= PAGE_COLS - N_HEADS, x, 0.0)
    sh = N_HEADS
    while sh < PAGE_COLS:
        tot = tot + pltpu.roll(tot, PAGE_COLS - sh, axis=1)
        sh *= 2
    carry = jnp.zeros((1, PAGE_COLS), jnp.float32)
    for p in range(n_pages):
        o_ref[:, p * PAGE_COLS:(p + 1) * PAGE_COLS] = -(x[p:p + 1] + carry)
        carry = carry + tot[p:p + 1]
    o_ref[:, n_pages * PAGE_COLS:] = -(_strided_cumsum(new_ref[...], N_HEADS) + carry[:, :NEW_COLS])


def paged_logf_cumsum(page_table, pool_logf, logf_new):
    n_seq, n_pages = page_table.shape
    width = n_pages * PAGE_COLS + NEW_COLS
    pspecs = [pl.BlockSpec((None, 1, PAGE_COLS), lambda b, pt, j=j: (pt[b, j], 0, 0)) for j in range(n_pages)]
    grid_spec = pltpu.PrefetchScalarGridSpec(
        num_scalar_prefetch=1,
        grid=(n_seq,),
        in_specs=pspecs + [pl.BlockSpec((None, 1, NEW_COLS), lambda b, pt: (b, 0, 0))],
        out_specs=pl.BlockSpec((None, 1, width), lambda b, pt: (b, 0, 0)),
        scratch_shapes=[pltpu.VMEM((n_pages, PAGE_COLS), jnp.float32)],
    )
    return pl.pallas_call(
        functools.partial(_paged_logf_kernel, n_pages=n_pages),
        grid_spec=grid_spec,
        out_shape=jax.ShapeDtypeStruct((n_seq, 1, width), jnp.float32),
        compiler_params=_params(("parallel",)),
        name="paged_logf_cumsum",
    )(page_table, *([pool_logf] * n_pages), logf_new)


def _paged_select_kernel(s_ref, o_ref, *, past_len, topk):
    score = s_ref[...]
    col = lax.broadcasted_iota(jnp.int32, score.shape, 2)
    qrow = lax.broadcasted_iota(jnp.int32, score.shape, 1)
    admissible = col <= past_len + qrow
    key = jnp.where(admissible, _sortable_key(score), INT_MIN)
    thr = _kth_largest_key(key, topk)
    o_ref[...] = jnp.where(admissible & (key >= thr), 1.0, 0.0)


def paged_select(scores, past_len, n_new, sb):
    n_seq, _, width = scores.shape
    topk = min(TOPK_MAX, (past_len + n_new) // 4)
    spec = pl.BlockSpec((sb, QPAD, width), lambda i: (i, 0, 0))
    return pl.pallas_call(
        functools.partial(_paged_select_kernel, past_len=past_len, topk=topk),
        grid=(n_seq // sb,),
        in_specs=[spec],
        out_specs=spec,
        out_shape=jax.ShapeDtypeStruct(scores.shape, jnp.float32),
        compiler_params=_params(("parallel",)),
        name="paged_select",
    )(scores)


def _branch_kernel(oa_ref, ob_ref, ga_ref, gb_ref, wa_ref, wb_ref, o_ref):
    ya = jnp.dot(oa_ref[...], wa_ref[...], preferred_element_type=jnp.float32)
    yb = jnp.dot(ob_ref[...], wb_ref[...], preferred_element_type=jnp.float32)
    o_ref[...] = _bf(ga_ref[...] * ya + gb_ref[...] * yb)


def merge_branches(oa, ob, gates, wa, wb, tm):
    n = oa.shape[0]
    d = wa.shape[1]
    return pl.pallas_call(
        _branch_kernel,
        grid=(n // tm,),
        in_specs=[pl.BlockSpec((tm, WIDTH), lambda i: (i, 0)),
                  pl.BlockSpec((tm, WIDTH), lambda i: (i, 0)),
                  pl.BlockSpec((tm, d), lambda i: (i, 0)),
                  pl.BlockSpec((tm, d), lambda i: (i, 1)),
                  pl.BlockSpec((WIDTH, d), lambda i: (0, 0)),
                  pl.BlockSpec((WIDTH, d), lambda i: (0, 0))],
        out_specs=pl.BlockSpec((tm, d), lambda i: (i, 0)),
        out_shape=jax.ShapeDtypeStruct((n, d), jnp.bfloat16),
        compiler_params=_params(("parallel",)),
        name="merge_branches",
    )(oa, ob, gates, gates, wa, wb)


def _out_proj_kernel(u_ref, x_ref, w_ref, g_ref, x2_ref, h2_ref):
    x2 = x_ref[...] + jnp.dot(u_ref[...], w_ref[...], preferred_element_type=jnp.float32)
    x2_ref[...] = x2
    ms = jnp.mean(x2 * x2, axis=-1, keepdims=True)
    h2_ref[...] = _bf(x2 * lax.rsqrt(ms + NORM_EPS) * g_ref[...])


def out_projection(u, x, w_out, gain, tm):
    n, d = x.shape
    row = pl.BlockSpec((tm, d), lambda i: (i, 0))
    return pl.pallas_call(
        _out_proj_kernel,
        grid=(n // tm,),
        in_specs=[row, row, pl.BlockSpec((d, d), lambda i: (0, 0)), pl.BlockSpec((1, d), lambda i: (0, 0))],
        out_specs=[row, row],
        out_shape=[jax.ShapeDtypeStruct((n, d), jnp.float32), jax.ShapeDtypeStruct((n, d), jnp.bfloat16)],
        compiler_params=_params(("parallel",)),
        name="out_projection",
    )(u, x, w_out, gain.reshape(1, d))


def _staircase_pairs():
    return [(r, c) for r in range(PEER_TOPK) for c in range(PEER_TOPK) if (r + 1) * (c + 1) <= PEER_TOPK]


def _peer_route_kernel(h_ref, wq_ref, ka_ref, kb_ref, thr_ref, e0_ref, s1_ref, e1_ref, x_scr, a_scr):
    tm = h_ref.shape[0]
    q = _bf(jnp.dot(h_ref[...], wq_ref[...], preferred_element_type=jnp.float32))
    sa = lax.dot_general(ka_ref[...], q, _NT, preferred_element_type=jnp.float32)
    sb = lax.dot_general(kb_ref[...], q, _NT, preferred_element_type=jnp.float32)
    x_scr[...] = sa.reshape(2, PEER_NKEYS, PEER_HEADS, tm)

    def extract(r, carry):
        for c in range(2):
            x = x_scr[c]
            m = jnp.max(x, axis=0)
            a_scr[c, r] = m
            x_scr[c] = jnp.where(x == m[None], -jnp.inf, x)
        return carry

    lax.fori_loop(0, PEER_TOPK, extract, 0)
    a0 = [a_scr[0, r] for r in range(PEER_TOPK)]
    a1 = [a_scr[1, r] for r in range(PEER_TOPK)]
    pairs = _staircase_pairs()
    cands = jnp.stack([a0[r] + a1[c] for r, c in pairs], axis=0)
    work = cands
    kth = None
    for r in range(PEER_TOPK + 1):
        m = jnp.max(work, axis=0)
        if r == PEER_TOPK - 1:
            kth = m
        work = jnp.where(work == m[None], -jnp.inf, work)
    tau = 0.5 * (kth + m)
    mx = a0[0] + a1[0]
    z = jnp.sum(jnp.where(cands >= tau[None], jnp.exp(cands - mx[None]), 0.0), axis=0)
    sb = sb.reshape(2, PEER_HEADS, PEER_NKEYS, tm)
    inv_z = 1.0 / z
    for h in range(PEER_HEADS):
        s0 = sb[0, h]
        s1 = sb[1, h]
        thr_ref[h] = tau[h:h + 1, :] - s0
        e0_ref[h] = jnp.exp(s0 - a0[0][h:h + 1, :]) * inv_z[h:h + 1, :]
        s1_ref[h] = s1
        e1_ref[h] = jnp.exp(s1 - a1[0][h:h + 1, :])


def peer_route(h2, wq, ka, kb, tm):
    n, d = h2.shape
    out = jax.ShapeDtypeStruct((PEER_HEADS, PEER_NKEYS, n), jnp.float32)
    ospec = pl.BlockSpec((PEER_HEADS, PEER_NKEYS, tm), lambda i: (0, 0, i))
    const = lambda shape: pl.BlockSpec(shape, lambda i: (0,) * len(shape))
    return pl.pallas_call(
        _peer_route_kernel,
        grid=(n // tm,),
        in_specs=[pl.BlockSpec((tm, d), lambda i: (i, 0)), const(wq.shape), const(ka.shape), const(kb.shape)],
        out_specs=[ospec] * 4,
        out_shape=[out] * 4,
        scratch_shapes=[pltpu.VMEM((2, PEER_NKEYS, PEER_HEADS, tm), jnp.float32),
                        pltpu.VMEM((2, PEER_TOPK, PEER_HEADS, tm), jnp.float32)],
        compiler_params=_params(("parallel",)),
        name="peer_route",
    )(h2, wq, ka, kb)


def _peer_expert_kernel(h_ref, x_ref, u_ref, vt_ref, thr_ref, e0_ref, s1_ref, e1_ref, o_ref, acc_ref, g_scr, *, te):
    e = pl.program_id(1)
    n_i = te // PEER_NKEYS
    blocks = thr_ref.shape[1] // n_i
    base = (e % blocks) * n_i

    @pl.when(e == 0)
    def _():
        acc_ref[...] = jnp.zeros(acc_ref.shape, jnp.float32)

    for ii in range(n_i):
        g = None
        for h in range(PEER_HEADS):
            thr = thr_ref[h, pl.ds(base + ii, 1), :]
            e0 = e0_ref[h, pl.ds(base + ii, 1), :]
            term = jnp.where(s1_ref[h] >= thr, e1_ref[h], 0.0) * e0
            g = term if g is None else g + term
        g_scr[ii * PEER_NKEYS:(ii + 1) * PEER_NKEYS, :] = g
    ht = lax.dot_general(u_ref[...], h_ref[...], _NT, preferred_element_type=jnp.float32)
    act = 0.5 * ht * (1.0 + lax.erf(ht * (2.0 ** -0.5)))
    wt = _bf(act * g_scr[...])
    acc_ref[...] += jnp.dot(vt_ref[...], wt, preferred_element_type=jnp.float32)

    @pl.when(e == pl.num_programs(1) - 1)
    def _():
        o_ref[...] = x_ref[...] + acc_ref[...].T


def peer_experts(h2, x2, u, vt, thr, e0, s1, e1, tm, te):
    n, d = h2.shape
    n_exp = u.shape[0]
    n_i = te // PEER_NKEYS
    ib = max(8, n_i)
    route_i = pl.BlockSpec((PEER_HEADS, ib, tm), lambda i, e: (0, (e * n_i) // ib, i))
    route_j = pl.BlockSpec((PEER_HEADS, PEER_NKEYS, tm), lambda i, e: (0, 0, i))
    row = pl.BlockSpec((tm, d), lambda i, e: (i, 0))
    return pl.pallas_call(
        functools.partial(_peer_expert_kernel, te=te),
        grid=(n // tm, n_exp // te),
        in_specs=[row, row,
                  pl.BlockSpec((te, d), lambda i, e: (e, 0)),
                  pl.BlockSpec((d, te), lambda i, e: (0, e)),
                  route_i, route_i, route_j, route_j],
        out_specs=row,
        out_shape=jax.ShapeDtypeStruct((n, d), jnp.float32),
        scratch_shapes=[pltpu.VMEM((d, tm), jnp.float32), pltpu.VMEM((te, tm), jnp.float32)],
        compiler_params=_params(("parallel", "arbitrary")),
        name="peer_experts",
    )(h2, x2, u, vt, thr, e0, s1, e1)


def _t5_bucket(rel):
    n = jnp.maximum(rel, 0)
    max_exact = N_BUCKETS // 2
    nf = jnp.maximum(n, 1).astype(jnp.float32)
    large = max_exact + (jnp.log(nf / max_exact) / math.log(MAX_DISTANCE / max_exact)
                         * (N_BUCKETS - max_exact)).astype(jnp.int32)
    large = jnp.minimum(large, N_BUCKETS - 1)
    return jnp.where(n < max_exact, n, large)


def _tile(n, pref):
    t = min(n, pref)
    assert n % t == 0, (n, pref)
    return t


def _group(x, weights, batch, seq):
    n, d = x.shape
    tm = _tile(n, 512)
    h = rmsnorm_bf16(x, weights["norm_mix"], tm)
    qa, ka, ka16, va, va16 = project_qkv(h, weights["w_a"], weights["gain_a"], tm)
    qb, kb, kb16, vb, vb16 = project_qkv(h, weights["w_b"], weights["gain_b"], tm)
    zero = jnp.zeros((1, WIDTH), jnp.float32)
    qi = project(h, weights["w_qi"], zero, tm, WIDTH, "plain", jnp.bfloat16)
    small = project(h, weights["w_small"], weights["b_small"], tm, SMALL_W, "small", jnp.float32)
    gates = project(h, weights["w_gate"], weights["b_gate"], tm, WIDTH, "sigmoid", jnp.float32)
    return dict(qa=qa, ka=ka, ka16=ka16, va=va, va16=va16, qb=qb, kb=kb, kb16=kb16, vb=vb, vb16=vb16,
                qi=qi, small=small, gates=gates)


def _finish(x, oa, ob, gates, weights):
    n = x.shape[0]
    u = merge_branches(oa, ob, gates, weights["w_branch_a"], weights["w_branch_b"], _tile(n, 256))
    x2, h2 = out_projection(u, x, weights["w_out"], weights["norm_ffn"], _tile(n, 256))
    thr, e0, s1, e1 = peer_route(h2, weights["peer_wq"], weights["peer_ka"], weights["peer_kb"], _tile(n, 256))
    return peer_experts(h2, x2, weights["peer_u"], weights["peer_vt"], thr, e0, s1, e1, _tile(n, 512), 512)


def _heads_first(a, n_seq, n_new):
    a = a.astype(jnp.float32).reshape(n_seq, n_new, N_HEADS, HEAD_DIM).transpose(0, 2, 1, 3)
    return jnp.pad(a, ((0, 0), (0, 0), (0, QPAD - n_new), (0, 0)))


def _heads_last(o, n_seq, n_new):
    o = o.reshape(n_seq, N_HEADS, QPAD, HEAD_DIM)[:, :, :n_new].transpose(0, 2, 1, 3)
    return _bf(o.reshape(n_seq * n_new, WIDTH))


def kernel(x_prompt, x_sample, cache_fox_k, cache_fox_v, cache_fox_logf, cache_dsa_k, cache_dsa_v, cache_idx_k,
           page_table, norm_mix, w_in, b_forget, b_gate, qk_gain, w_branch_a, w_branch_b, w_out, rel_bias,
           norm_ffn, peer_w_query, peer_sub_keys, peer_u, peer_v):
    bp, tp, d = x_prompt.shape
    bs, ts, _ = x_sample.shape
    n_pages = page_table.shape[1]
    past_len = n_pages * PAGE
    depth = w_in.shape[0]
    assert depth == 1, "one layer per call"
    l = 0

    offs = np.cumsum([0, WIDTH, WIDTH, WIDTH, N_HEADS, WIDTH, WIDTH, WIDTH, H_IDX * D_IDX, D_IDX, H_IDX, d, d])
    w = w_in[l]
    col = lambda a, b: w[:, int(offs[a]):int(offs[b])]
    pad = SMALL_W - (D_IDX + H_IDX + N_HEADS)
    gains = jnp.tile(qk_gain[l], (1, N_HEADS))
    weights = dict(
        norm_mix=norm_mix[l], norm_ffn=norm_ffn[l],
        w_a=_bf(col(0, 3)), w_b=_bf(col(4, 7)), gain_a=gains[0:2], gain_b=gains[2:4],
        w_qi=_bf(col(7, 8)),
        w_small=_bf(jnp.concatenate([col(8, 9), col(9, 10), col(3, 4), jnp.zeros((d, pad), w.dtype)], axis=1)),
        b_small=jnp.concatenate([jnp.zeros((D_IDX + H_IDX,), jnp.float32), b_forget[l],
                                 jnp.zeros((pad,), jnp.float32)]).reshape(1, SMALL_W),
        w_gate=_bf(col(10, 12)), b_gate=b_gate[l].reshape(1, 2 * d),
        w_branch_a=_bf(w_branch_a[l]), w_branch_b=_bf(w_branch_b[l]), w_out=_bf(w_out[l]),
        peer_wq=_bf(peer_w_query[l]), peer_u=_bf(peer_u[l]), peer_vt=_bf(peer_v[l].T),
    )
    sk = peer_sub_keys[l]
    eye_h = jnp.eye(PEER_HEADS, dtype=sk.dtype)
    eye_c = jnp.eye(2, dtype=sk.dtype)
    n_rows = 2 * PEER_NKEYS * PEER_HEADS
    weights["peer_ka"] = _bf(jnp.einsum('hckd,hg,ce->ckhged', sk, eye_h, eye_c).reshape(n_rows, -1))
    weights["peer_kb"] = _bf(jnp.einsum('hckd,hg,ce->chkged', sk, eye_h, eye_c).reshape(n_rows, -1))

    def t5_of(rel):
        return jnp.moveaxis(rel_bias[_t5_bucket(rel)].astype(jnp.float32), -1, 0)

    xp = x_prompt.reshape(bp * tp, d)
    gp = _group(xp, weights, bp, tp)
    logf_p = gp["small"][:, FA_OFF:FA_OFF + N_HEADS].reshape(bp, tp, N_HEADS)
    ki_p = gp["small"][:, KI_OFF:KI_OFF + D_IDX].reshape(bp, tp, D_IDX)
    negc_p = -jnp.swapaxes(jnp.cumsum(logf_p, axis=1), 1, 2)
    ta = _tile(tp, 256)
    oa_p = fox_prompt_attention(gp["qa"], gp["ka16"], gp["va16"], negc_p, bp, tp, ta)
    mask_p = dsa_prompt_select(gp["qi"], gp["small"], bp, tp, ta)
    ar = jnp.arange(ta, dtype=jnp.int32)
    rel_tiles = jnp.stack([ar[:, None] - ar[None, :], ta + ar[:, None] - ar[None, :],
                           jnp.full((ta, ta), 2 * ta, jnp.int32)])
    t5_tiles = t5_of(rel_tiles).transpose(1, 0, 2, 3)
    ob_p = dsa_prompt_attention(gp["qb"], gp["kb16"], gp["vb16"], mask_p, t5_tiles, bp, tp, ta)
    y_prompt = _finish(xp, oa_p, ob_p, gp["gates"], weights).reshape(bp, tp, d)

    xs = x_sample.reshape(bs * ts, d)
    gs = _group(xs, weights, bs, ts)
    logf_s = gs["small"][:, FA_OFF:FA_OFF + N_HEADS].reshape(bs, ts, N_HEADS)
    ki_s = gs["small"][:, KI_OFF:KI_OFF + D_IDX].reshape(bs, ts, D_IDX)
    wi_s = gs["small"][:, WI_OFF:WI_OFF + H_IDX].reshape(bs, ts, H_IDX)
    width = (n_pages + 1) * PAGE
    pool = lambda c: c.reshape(c.shape[1:])
    q_rows = lambda a: _heads_first(a, bs, ts).reshape(bs, ROWS, HEAD_DIM)
    new_rows = lambda a: jnp.pad(a.reshape(bs, ts, N_HEADS, HEAD_DIM), ((0, 0), (0, NEWP - ts), (0, 0), (0, 0)))
    logf_new = jnp.pad(logf_s, ((0, 0), (0, NEWP - ts), (0, 0))).reshape(bs, 1, NEW_COLS)
    pool_logf = cache_fox_logf.astype(jnp.float32).reshape(-1, 1, PAGE_COLS)
    negc_s = paged_logf_cumsum(page_table, pool_logf, logf_new)
    oa_s = paged_attention(page_table, q_rows(gs["qa"]), pool(cache_fox_k), pool(cache_fox_v),
                           new_rows(gs["ka"]), new_rows(gs["va"]), ts, col_bias=negc_s)
    qi_s = gs["qi"].astype(jnp.float32).reshape(bs, ts, H_IDX, D_IDX).transpose(0, 2, 1, 3)
    qi_s = jnp.pad(qi_s, ((0, 0), (0, 0), (0, QPAD - ts), (0, 0))).reshape(bs, H_IDX * QPAD, D_IDX)
    w_s = jnp.pad(wi_s.transpose(0, 2, 1), ((0, 0), (0, 0), (0, QPAD - ts))).reshape(bs, H_IDX * QPAD, 1)
    ki_new = jnp.pad(ki_s, ((0, 0), (0, PAGE - ts), (0, 0)))
    scores_p, scores_n = paged_index_scores(page_table, qi_s, w_s, pool(cache_idx_k), ki_new)
    keep_s = paged_select(jnp.concatenate([scores_p, scores_n], axis=-1), past_len, ts, _tile(bs, 16))
    rel_s = past_len + jnp.arange(QPAD, dtype=jnp.int32)[:, None] - jnp.arange(width, dtype=jnp.int32)[None, :]
    t5_all = jnp.repeat(t5_of(rel_s).reshape(ROWS, width), N_HEADS, axis=1)
    t5_far = jnp.repeat(rel_bias[N_BUCKETS - 1].astype(jnp.float32), QPAD).reshape(ROWS, 1)
    t5_s = (t5_far, t5_all[:, (n_pages - 1) * PAGE_COLS:n_pages * PAGE_COLS],
            t5_all[:, n_pages * PAGE_COLS:n_pages * PAGE_COLS + NEW_COLS])
    ob_s = paged_attention(page_table, q_rows(gs["qb"]), pool(cache_dsa_k), pool(cache_dsa_v),
                           new_rows(gs["kb"]), new_rows(gs["vb"]), ts, keep=keep_s, t5=t5_s)
    y_sample = _finish(xs, _heads_last(oa_s, bs, ts), _heads_last(ob_s, bs, ts), gs["gates"],
                       weights).reshape(bs, ts, d)

    def rows(g, b, t):
        kv = lambda a: a.reshape(1, b, t, N_HEADS, HEAD_DIM)
        return (kv(g["ka"]), kv(g["va"]), None, kv(g["kb"]), kv(g["vb"]), None)

    p_rows = rows(gp, bp, tp)
    s_rows = rows(gs, bs, ts)
    return (y_prompt, y_sample,
            p_rows[0], p_rows[1], logf_p[None], p_rows[3], p_rows[4], ki_p[None],
            s_rows[0], s_rows[1], logf_s[None], s_rows[3], s_rows[4], ki_s[None])
```
